```python
import jax
import jax.numpy as jnp
from jax import lax
import numpy as np

D_MODEL = 1024
BATCH = 32
SEQ = 2048
DEPTH = 4

CTX_LEN = 256
GRID_W = 64
RMS_EPS = 1e-6

MLA_HEADS = 8
MLA_Q_RANK = 256
MLA_KV_RANK = 128
MLA_NOPE = 64
MLA_ROPE = 32
MLA_V = 64
MLA_WIDTH = MLA_HEADS * MLA_V
MLA_SCALE = (MLA_NOPE + MLA_ROPE) ** -0.5
ROPE_PAIRS = MLA_ROPE // 4
ROPE_BASE = 10000.0
Q_BLOCK = 128

HG_HEADS = 8
HG_DK = 64
HG_DV = 64
HG_KWIDTH = HG_HEADS * HG_DK
HG_WIDTH = HG_HEADS * HG_DV
HG_CHUNK = 16

CV_WIDTH = 512
CV_K = 3

N_BRANCH = 3
BR_WIDTH = 512

IN_SIZES = (MLA_Q_RANK, MLA_KV_RANK, MLA_ROPE, MLA_WIDTH,
            HG_KWIDTH, HG_WIDTH, HG_KWIDTH, HG_KWIDTH, HG_WIDTH,
            CV_WIDTH, CV_WIDTH, CV_WIDTH, CV_WIDTH,
            N_BRANCH * D_MODEL)
N_IN = sum(IN_SIZES)

kernel_name = "hybrid_mla_hgrn2_shortconv_dit"


def rms_norm(x, g):
    xf = x.astype(jnp.float32)
    y = xf * lax.rsqrt(jnp.mean(xf * xf, axis=-1, keepdims=True) + RMS_EPS)
    return (y * g.astype(jnp.float32)).astype(x.dtype)


def split_columns(z):
    cuts = np.cumsum(np.array(IN_SIZES))[:-1].tolist()
    return jnp.split(z, cuts, axis=-1)


def axial_rope_tables(n_lat, dtype):
    rows = n_lat // GRID_W
    row_id = jnp.broadcast_to(jnp.arange(rows, dtype=jnp.float32)[:, None], (rows, GRID_W)).reshape(-1)
    col_id = jnp.broadcast_to(jnp.arange(GRID_W, dtype=jnp.float32)[None, :], (rows, GRID_W)).reshape(-1)
    inv_freq = jnp.power(ROPE_BASE, -jnp.arange(ROPE_PAIRS, dtype=jnp.float32) / ROPE_PAIRS)
    ang = jnp.stack([row_id[:, None] * inv_freq, col_id[:, None] * inv_freq], axis=1)
    return jnp.cos(ang)[:, None].astype(dtype), jnp.sin(ang)[:, None].astype(dtype)


def apply_axial_rope(x, cos, sin):
    xs = x.reshape(x.shape[:-1] + (2, 2, ROPE_PAIRS))
    x1, x2 = xs[..., 0, :], xs[..., 1, :]
    rot = jnp.stack([x1 * cos - x2 * sin, x1 * sin + x2 * cos], axis=-2)
    return rot.reshape(x.shape)


def mla_attend(q_nope, q_rope, k_nope, k_rope, v):
    s = jnp.einsum('bqhd,bkhd->bhqk', q_nope, k_nope) + jnp.einsum('bqhr,bkr->bhqk', q_rope, k_rope)
    p = jax.nn.softmax(s.astype(jnp.float32) * MLA_SCALE, axis=-1).astype(v.dtype)
    return jnp.einsum('bhqk,bkhd->bqhd', p, v)


def mla_branch(cq, ckv, kr, q_norm_g, kv_norm_g, w_uq, w_ukv, n_ctx, with_ctx):
    bsz, n_tok, _ = cq.shape
    n_lat = n_tok - n_ctx
    q = (rms_norm(cq, q_norm_g) @ w_uq).reshape(bsz, n_tok, MLA_HEADS, MLA_NOPE + MLA_ROPE)
    kv = (rms_norm(ckv, kv_norm_g) @ w_ukv).reshape(bsz, n_tok, MLA_HEADS, MLA_NOPE + MLA_V)
    q_nope, q_rope = q[..., :MLA_NOPE], q[..., MLA_NOPE:]
    k_nope, v = kv[..., :MLA_NOPE], kv[..., MLA_NOPE:]
    cos, sin = axial_rope_tables(n_lat, q.dtype)
    q_rope_lat = apply_axial_rope(q_rope[:, n_ctx:], cos, sin)
    k_rope = jnp.concatenate([kr[:, :n_ctx], apply_axial_rope(kr[:, n_ctx:, None], cos, sin)[:, :, 0]], axis=1)
    n_blk = n_lat // Q_BLOCK
    qn_blk = q_nope[:, n_ctx:].reshape(bsz, n_blk, Q_BLOCK, MLA_HEADS, MLA_NOPE).swapaxes(0, 1)
    qr_blk = q_rope_lat.reshape(bsz, n_blk, Q_BLOCK, MLA_HEADS, MLA_ROPE).swapaxes(0, 1)
    o_lat = lax.map(lambda qs: mla_attend(qs[0], qs[1], k_nope, k_rope, v), (qn_blk, qr_blk))
    o_lat = o_lat.swapaxes(0, 1).reshape(bsz, n_lat, MLA_WIDTH)
    if not with_ctx:
        return o_lat
    o_ctx = mla_attend(q_nope[:, :n_ctx], q_rope[:, :n_ctx], k_nope[:, :n_ctx], k_rope[:, :n_ctx], v[:, :n_ctx])
    return jnp.concatenate([o_ctx.reshape(bsz, n_ctx, MLA_WIDTH), o_lat], axis=1)


def chunked_gated_scan(q, k, v, log_f, s0):
    bsz, n_tok, heads, _ = q.shape
    n_chunk = n_tok // HG_CHUNK
    to_chunks = lambda a: a.reshape(bsz, n_chunk, HG_CHUNK, heads, a.shape[-1]).transpose(1, 0, 3, 2, 4)
    mask = jnp.tril(jnp.ones((HG_CHUNK, HG_CHUNK), dtype=bool))

    def step(s, inp):
        qc, kc, vc, gc = inp
        b = jnp.cumsum(gc, axis=-2)
        b_last = b[..., -1:, :]
        q_dec = qc * jnp.exp(b)
        k_dec = kc * jnp.exp(-b)
        a = jnp.where(mask, jnp.einsum('bhtd,bhsd->bhts', q_dec, k_dec), 0.0)
        o = jnp.einsum('bhts,bhsv->bhtv', a, vc) + jnp.einsum('bhtd,bhdv->bhtv', q_dec, s)
        s_new = jnp.exp(b_last[..., 0, :])[..., None] * s + jnp.einsum('bhsd,bhsv->bhdv', kc * jnp.exp(b_last - b), vc)
        return s_new, o

    s_fin, o = lax.scan(step, s0, (to_chunks(q), to_chunks(k), to_chunks(v), to_chunks(log_f)))
    return o.transpose(1, 0, 3, 2, 4).reshape(bsz, n_tok, heads, v.shape[-1]), s_fin


def hgrn2_branch(q, i, f_fwd, f_bwd, lb, norm_g, n_ctx):
    bsz, n_tok, _ = q.shape
    heads = lambda a, d: a.astype(jnp.float32).reshape(bsz, n_tok, HG_HEADS, d)
    qh, vh = heads(q, HG_DK), heads(i, HG_DV)
    s0 = jnp.zeros((bsz, HG_HEADS, HG_DK, HG_DV), jnp.float32)

    def gates(f_logit, lb_d):
        z = heads(f_logit, HG_DK)
        lb_h = lb_d.reshape(HG_HEADS, HG_DK)
        log_f = jnp.logaddexp(jnp.log(lb_h), jnp.log1p(-lb_h) + jax.nn.log_sigmoid(z))
        k = (1.0 - lb_h) * jax.nn.sigmoid(-z)
        return k, log_f

    k_f, g_f = gates(f_fwd, lb[0])
    o_fwd, _ = chunked_gated_scan(qh, k_f, vh, g_f, s0)
    flip = lambda a: jnp.concatenate([jnp.flip(a[:, :n_ctx], axis=1), jnp.flip(a[:, n_ctx:], axis=1)], axis=1)
    k_b, g_b = gates(f_bwd, lb[1])
    o_bwd, _ = chunked_gated_scan(flip(qh), flip(k_b), flip(vh), flip(g_b), s0)
    o = o_fwd + flip(o_bwd)
    o = rms_norm(o, norm_g.reshape(HG_HEADS, HG_DV))
    return o.reshape(bsz, n_tok, HG_WIDTH).astype(q.dtype)


def short_conv(u, w, b):
    up = jnp.pad(u, ((0, 0), (1, 1), (0, 0)))
    return up[:, :-2] * w[0] + up[:, 1:-1] * w[1] + up[:, 2:] * w[2] + b


def conv_branch(xin, bg, cg, w, b, n_ctx, with_ctx):
    u = cg * xin
    y_lat = bg[:, n_ctx:] * short_conv(u[:, n_ctx:], w, b)
    if not with_ctx:
        return y_lat
    y_ctx = bg[:, :n_ctx] * short_conv(u[:, :n_ctx], w, b)
    return jnp.concatenate([y_ctx, y_lat], axis=1)


def setup_inputs(seed: int = 0) -> dict:
    key = jax.random.key(seed)
    ks = jax.random.split(key, 19)
    nrm = lambda k, shape, s: jax.random.normal(k, shape, jnp.float32) * s
    return {
        "x": nrm(ks[0], (BATCH, SEQ, D_MODEL), 1.0),
        "c": nrm(ks[1], (BATCH, D_MODEL), 1.0),
        "ctx": nrm(ks[2], (BATCH, CTX_LEN, D_MODEL), 1.0),
        "c_ctx": nrm(ks[3], (D_MODEL,), 1.0),
        "ada_w": nrm(ks[4], (DEPTH, D_MODEL, 3 * D_MODEL), 0.5 * D_MODEL ** -0.5),
        "ada_b": nrm(ks[5], (DEPTH, 3 * D_MODEL), 0.02),
        "norm_g": 1.0 + nrm(ks[6], (DEPTH, D_MODEL), 0.02),
        "w_in": nrm(ks[7], (DEPTH, D_MODEL, N_IN), D_MODEL ** -0.5),
        "mla_q_norm_g": 1.0 + nrm(ks[8], (DEPTH, MLA_Q_RANK), 0.02),
        "mla_kv_norm_g": 1.0 + nrm(ks[9], (DEPTH, MLA_KV_RANK), 0.02),
        "mla_w_uq": nrm(ks[10], (DEPTH, MLA_Q_RANK, MLA_HEADS * (MLA_NOPE + MLA_ROPE)), MLA_Q_RANK ** -0.5),
        "mla_w_ukv": nrm(ks[11], (DEPTH, MLA_KV_RANK, MLA_HEADS * (MLA_NOPE + MLA_V)), MLA_KV_RANK ** -0.5),
        "hg_lb_logits": nrm(ks[12], (DEPTH, 2, HG_KWIDTH), 0.5),
        "hg_norm_g": 1.0 + nrm(ks[13], (DEPTH, HG_WIDTH), 0.02),
        "conv_w": nrm(ks[14], (DEPTH, CV_K, CV_WIDTH), CV_K ** -0.5),
        "conv_b": nrm(ks[15], (DEPTH, CV_WIDTH), 0.02),
        "w_branch": nrm(ks[16], (DEPTH, N_BRANCH, BR_WIDTH, D_MODEL), BR_WIDTH ** -0.5),
        "w_out": nrm(ks[17], (DEPTH, D_MODEL, D_MODEL), D_MODEL ** -0.5),
        "final_norm_g": 1.0 + nrm(ks[18], (D_MODEL,), 0.02),
    }


def reference(x, c, ctx, c_ctx, ada_w, ada_b, norm_g, w_in, mla_q_norm_g, mla_kv_norm_g, mla_w_uq, mla_w_ukv,
              hg_lb_logits, hg_norm_g, conv_w, conv_b, w_branch, w_out, final_norm_g):
    n_ctx = ctx.shape[1]
    n_lat = x.shape[1]
    lb_all = jnp.cumsum(jax.nn.softmax(hg_lb_logits.astype(jnp.float32), axis=0), axis=0)
    lb_all = lb_all - lb_all[0:1]
    silu_c = jax.nn.silu(c)
    silu_cc = jax.nn.silu(c_ctx)
    h_lat, h_ctx = x, ctx
    for l in range(DEPTH):
        last = l == DEPTH - 1
        lo = n_ctx if last else 0
        mod = silu_c @ ada_w[l] + ada_b[l]
        mod_c = silu_cc @ ada_w[l] + ada_b[l]
        shift, scale, gate = jnp.split(mod[:, None, :], 3, axis=-1)
        shift_c, scale_c, gate_c = jnp.split(mod_c, 3, axis=-1)
        u = jnp.concatenate([rms_norm(h_ctx, norm_g[l]) * (1.0 + scale_c) + shift_c,
                             rms_norm(h_lat, norm_g[l]) * (1.0 + scale) + shift], axis=1)
        z = u @ w_in[l]
        (cq, ckv, kr, g_mla, hq, hi, hf_fwd, hf_bwd, g_hg, cx, cb, cc, g_cv, br_gate) = split_columns(z)
        y_mla = mla_branch(cq, ckv, kr, mla_q_norm_g[l], mla_kv_norm_g[l], mla_w_uq[l], mla_w_ukv[l],
                           n_ctx, not last) * jax.nn.silu(g_mla[:, lo:])
        y_hg = hgrn2_branch(hq, hi, hf_fwd, hf_bwd, lb_all[l], hg_norm_g[l], n_ctx)[:, lo:] * jax.nn.silu(g_hg[:, lo:])
        y_cv = conv_branch(cx, cb, cc, conv_w[l], conv_b[l], n_ctx, not last) * jax.nn.silu(g_cv[:, lo:])
        br_gate = jax.nn.sigmoid(br_gate[:, lo:])
        merged = (br_gate[..., :D_MODEL] * (y_mla @ w_branch[l, 0])
                  + br_gate[..., D_MODEL:2 * D_MODEL] * (y_hg @ w_branch[l, 1])
                  + br_gate[..., 2 * D_MODEL:] * (y_cv @ w_branch[l, 2]))
        out = merged @ w_out[l]
        h_lat = h_lat + gate * out[:, n_ctx - lo:]
        if not last:
            h_ctx = h_ctx + gate_c * out[:, :n_ctx]
    return rms_norm(h_lat, final_norm_g)
```

```python
import functools

import numpy as np
import jax
import jax.numpy as jnp
from jax import lax
from jax.experimental import pallas as pl
from jax.experimental.pallas import tpu as pltpu

F32 = jnp.float32
BF16 = jnp.bfloat16

RMS_EPS = 1e-6
GRID_W = 64
ROPE_BASE = 10000.0
MLA_HEADS = 8
MLA_NOPE = 64
MLA_ROPE = 32
MLA_V = 64
MLA_SLOT = 128
MLA_SCALE = (MLA_NOPE + MLA_ROPE) ** -0.5
HG_HEADS = 8
HG_DK = 64
HG_DV = 64
HG_GROUP = 256
HG_CHUNK = 32
CV_K = 3

V7X_VMEM_BYTES = 64 * 1024 * 1024
ROW_TILE = 256


def _vmem_limit(nbytes):
    return int(min(nbytes + (12 << 20), V7X_VMEM_BYTES - (6 << 20)))


def _dot(a, b):
    return jnp.dot(a, b, preferred_element_type=F32)


def _dot_nt(a, b):
    return lax.dot_general(a, b, (((1,), (1,)), ((), ())), preferred_element_type=F32)


def _dot_tn(a, b):
    return lax.dot_general(a, b, (((0,), (0,)), ((), ())), preferred_element_type=F32)


def _sigmoid(x):
    return 1.0 / (1.0 + jnp.exp(-x))


def _silu(x):
    return x * _sigmoid(x)


def _split2(x):
    hi = x.astype(BF16)
    lo = (x - hi.astype(F32)).astype(BF16)
    return hi, lo


def _mod_kernel(c_ref, w_ref, b_ref, o_ref):
    s = _silu(c_ref[...])
    o_ref[0] = jnp.dot(s, w_ref[0], preferred_element_type=F32, precision=lax.Precision.HIGHEST) + b_ref[0]


def _modulation(cs, ada_w, ada_b):
    depth, d, d3 = ada_w.shape
    rows = cs.shape[0]
    nb = d3 // d
    return pl.pallas_call(
        _mod_kernel,
        out_shape=jax.ShapeDtypeStruct((depth, rows, d3), F32),
        grid=(depth, nb),
        in_specs=[
            pl.BlockSpec((rows, d), lambda l, j: (0, 0)),
            pl.BlockSpec((1, d, d), lambda l, j: (l, 0, j)),
            pl.BlockSpec((1, 1, d), lambda l, j: (l, 0, j)),
        ],
        out_specs=pl.BlockSpec((1, rows, d), lambda l, j: (l, 0, j)),
        compiler_params=pltpu.CompilerParams(
            dimension_semantics=("arbitrary", "arbitrary"),
            vmem_limit_bytes=_vmem_limit(2 * d * d * 4 + 4 * rows * d * 4),
        ),
        name="adaln_mod",
    )(cs, ada_w, ada_b.reshape(depth, 1, d3))


def _lb_kernel(x_ref, o_ref):
    x = x_ref[...]
    m = jnp.max(x, axis=0, keepdims=True)
    e = jnp.exp(x - m)
    p = e / jnp.sum(e, axis=0, keepdims=True)
    depth = x.shape[0]
    rows = [p[0:1]]
    for l in range(1, depth):
        rows.append(rows[-1] + p[l:l + 1])
    cs = jnp.concatenate(rows, axis=0)
    o_ref[...] = cs - cs[0:1]


def _lower_bounds(hg_lb_logits):
    depth = hg_lb_logits.shape[0]
    flat = hg_lb_logits.reshape(depth, -1).astype(F32)
    return pl.pallas_call(
        _lb_kernel,
        out_shape=jax.ShapeDtypeStruct(flat.shape, F32),
        name="hgrn_lower_bounds",
    )(flat)


def _norm_kernel(h_ref, modb_ref, modc_ref, g_ref, u_ref, *, n_lat, tm):
    d = h_ref.shape[-1]
    x = h_ref[0]
    ms = jnp.mean(x * x, axis=-1, keepdims=True)
    y = x * lax.rsqrt(ms + RMS_EPS) * g_ref[...]
    row = pl.program_id(1) * tm + lax.broadcasted_iota(jnp.int32, (tm, 1), 0)
    is_lat = row < n_lat
    mb = modb_ref[0]
    mc = modc_ref[0]
    shift = jnp.where(is_lat, mb[:, 0:d], mc[:, 0:d])
    scale = jnp.where(is_lat, mb[:, d:2 * d], mc[:, d:2 * d])
    u_ref[0] = (y * (1.0 + scale) + shift).astype(BF16)


def _norm_mod(h, mod3, norm_g, n_lat, tm):
    bsz, n, d = h.shape
    ctx_row = bsz
    return pl.pallas_call(
        functools.partial(_norm_kernel, n_lat=n_lat, tm=tm),
        out_shape=jax.ShapeDtypeStruct((bsz, n, d), BF16),
        grid=(bsz, n // tm),
        in_specs=[
            pl.BlockSpec((1, tm, d), lambda b, j: (b, j, 0)),
            pl.BlockSpec((1, 1, 3 * d), lambda b, j: (b, 0, 0)),
            pl.BlockSpec((1, 1, 3 * d), lambda b, j: (ctx_row, 0, 0)),
            pl.BlockSpec((1, d), lambda b, j: (0, 0)),
        ],
        out_specs=pl.BlockSpec((1, tm, d), lambda b, j: (b, j, 0)),
        compiler_params=pltpu.CompilerParams(
            dimension_semantics=("arbitrary", "arbitrary"),
            vmem_limit_bytes=_vmem_limit(2 * tm * d * 6 + 4 * tm * d * 4),
        ),
        name="norm_mod",
    )(h, mod3, mod3, norm_g.reshape(1, d))


def _mla_kernel(u_ref, w_ref, gq_ref, gkv_ref, wuq_ref, wukn_ref, wvt_ref, cos_ref, sin_ref, out_ref,
                q_s, k_s, vt_s, g_s, ot_s, *, n_lat, q_rank, kv_rank):
    n = u_ref.shape[1]
    n_ctx = n - n_lat
    hs = MLA_HEADS * MLA_SLOT
    width = MLA_HEADS * MLA_V
    c_kv = q_rank
    c_kr = q_rank + kv_rank
    c_krs = c_kr + MLA_SLOT
    c_g = c_krs + MLA_SLOT

    for t in range(n // ROW_TILE):
        rows = slice(t * ROW_TILE, (t + 1) * ROW_TILE)
        z = _dot(u_ref[0, rows, :], w_ref[...])
        cos_t = cos_ref[rows, :]
        sin_t = sin_ref[rows, :]
        cos8 = jnp.concatenate([cos_t] * MLA_HEADS, axis=1)
        sin8 = jnp.concatenate([sin_t] * MLA_HEADS, axis=1)
        cq = z[:, 0:q_rank]
        cqn = (cq * lax.rsqrt(jnp.mean(cq * cq, axis=-1, keepdims=True) + RMS_EPS) * gq_ref[...]).astype(BF16)
        q2 = _dot(cqn, wuq_ref[...])
        q = (q2[:, 0:hs] * cos8 + q2[:, hs:2 * hs] * sin8).astype(BF16)
        ckv = z[:, c_kv:c_kr]
        kvn = (ckv * lax.rsqrt(jnp.mean(ckv * ckv, axis=-1, keepdims=True) + RMS_EPS) * gkv_ref[...]).astype(BF16)
        kn = _dot(kvn, wukn_ref[...])
        kr = z[:, c_kr:c_krs] * cos_t + z[:, c_krs:c_g] * sin_t
        k = (kn + jnp.concatenate([kr] * MLA_HEADS, axis=1)).astype(BF16)
        for h in range(MLA_HEADS):
            q_s[h, rows, :] = q[:, h * MLA_SLOT:(h + 1) * MLA_SLOT]
            k_s[h, rows, :] = k[:, h * MLA_SLOT:(h + 1) * MLA_SLOT]
        vt = _dot_nt(wvt_ref[...], kvn).astype(BF16)
        for h in range(MLA_HEADS):
            vt_s[h, :, rows] = vt[h * MLA_V:(h + 1) * MLA_V, :]
        g_s[rows, :] = _silu(z[:, c_g:c_g + width])

    def attend(r0, k0, nk):
        rows = pl.ds(r0, ROW_TILE)

        def head(h, carry):
            st = _dot_nt(k_s[h, pl.ds(k0, nk), :], q_s[h, rows, :]) * MLA_SCALE
            m = jnp.max(st, axis=0, keepdims=True)
            p = jnp.exp(st - m)
            l = jnp.sum(p, axis=0, keepdims=True)
            ot = _dot(vt_s[h, :, pl.ds(k0, nk)], p.astype(BF16))
            ot_s[h] = ot * (1.0 / l)
            return carry

        lax.fori_loop(0, MLA_HEADS, head, 0)
        o = jnp.concatenate([ot_s[h] for h in range(MLA_HEADS)], axis=0).T
        out_ref[0, rows, :] = (o * g_s[rows, :]).astype(BF16)

    def lat_tile(t, carry):
        attend(pl.multiple_of(t * ROW_TILE, ROW_TILE), 0, n)
        return carry

    lax.fori_loop(0, n_lat // ROW_TILE, lat_tile, 0)
    for t in range(n_ctx // ROW_TILE):
        attend(n_lat + t * ROW_TILE, n_lat, n_ctx)


def _mla(u, w_mla, gq, gkv, wuq2, wukn, wvt, cos_t, sin_t, n_lat):
    bsz, n, d = u.shape
    q_rank, kv_rank = gq.shape[-1], gkv.shape[-1]
    width = MLA_HEADS * MLA_V
    hs = MLA_HEADS * MLA_SLOT
    est = (2 * n * d * 2 + 2 * w_mla.size * 2 + 2 * (wuq2.size + wukn.size + wvt.size) * 2
           + 4 * n * MLA_SLOT * 4 + 2 * n * width * 2
           + 2 * n * hs * 2 + n * width * 2 + n * width * 4 + width * ROW_TILE * 4
           + 3 * n * ROW_TILE * 4)
    full = lambda a: pl.BlockSpec(a.shape, lambda b: (0,) * a.ndim)
    return pl.pallas_call(
        functools.partial(_mla_kernel, n_lat=n_lat, q_rank=q_rank, kv_rank=kv_rank),
        out_shape=jax.ShapeDtypeStruct((bsz, n, width), BF16),
        grid=(bsz,),
        in_specs=[
            pl.BlockSpec((1, n, d), lambda b: (b, 0, 0)),
            full(w_mla), full(gq), full(gkv), full(wuq2), full(wukn), full(wvt), full(cos_t), full(sin_t),
        ],
        out_specs=pl.BlockSpec((1, n, width), lambda b: (b, 0, 0)),
        scratch_shapes=[
            pltpu.VMEM((MLA_HEADS, n, MLA_SLOT), BF16),
            pltpu.VMEM((MLA_HEADS, n, MLA_SLOT), BF16),
            pltpu.VMEM((MLA_HEADS, MLA_V, n), BF16),
            pltpu.VMEM((n, width), F32),
            pltpu.VMEM((MLA_HEADS, MLA_V, ROW_TILE), F32),
        ],
        compiler_params=pltpu.CompilerParams(
            dimension_semantics=("arbitrary",), vmem_limit_bytes=_vmem_limit(est)),
        name="mla_mixer",
    )(u, w_mla, gq, gkv, wuq2, wukn, wvt, cos_t, sin_t)


def _hgrn_kernel(u_ref, w_ref, lb_ref, gn_ref, out_ref,
                 o_s, st_s, q_s, v_s, k_s, b_s, *, n_lat):
    n = u_ref.shape[1]
    kw = HG_HEADS * HG_DK
    vw = HG_HEADS * HG_DV
    n_tiles = n // ROW_TILE
    lat_tiles = n_lat // ROW_TILE
    ctx_tiles = n_tiles - lat_tiles
    n_chunk = ROW_TILE // HG_CHUNK
    n_grp = kw // HG_GROUP
    heads_per_grp = HG_GROUP // HG_DK
    stack = heads_per_grp * HG_CHUNK

    ti = lax.broadcasted_iota(jnp.int32, (ROW_TILE, ROW_TILE), 0)
    si = lax.broadcasted_iota(jnp.int32, (ROW_TILE, ROW_TILE), 1)
    same = (ti // HG_CHUNK) == (si // HG_CHUNK)
    tri = (jnp.where(same & (si <= ti), 1.0, 0.0).astype(BF16),
           jnp.where(same & (si >= ti), 1.0, 0.0).astype(BF16))
    hr = lax.broadcasted_iota(jnp.int32, (stack, HG_GROUP), 0) // HG_CHUNK
    hc = lax.broadcasted_iota(jnp.int32, (stack, HG_GROUP), 1) // HG_DK
    head_mask = jnp.where(hr == hc, 1.0, 0.0).astype(BF16)
    at = lax.broadcasted_iota(jnp.int32, (HG_CHUNK, stack), 0)
    as_ = lax.broadcasted_iota(jnp.int32, (HG_CHUNK, stack), 1) % HG_CHUNK
    causal = (as_ <= at, as_ >= at)
    ref_row = (HG_CHUNK // 2 - 1, HG_CHUNK // 2)
    end_row = (HG_CHUNK - 1, 0)

    lb_all = lb_ref[...]

    st_s[...] = jnp.zeros_like(st_s)

    def tile_step(i, carry):
        t_fwd = jnp.where(i < ctx_tiles, lat_tiles + i, i - ctx_tiles)
        t_bwd = n_tiles - 1 - i
        tile_of = (t_fwd, t_bwd)
        for d in range(2):
            r0 = pl.multiple_of(tile_of[d] * ROW_TILE, ROW_TILE)
            ut = u_ref[0, pl.ds(r0, ROW_TILE), :]
            qv = _dot(ut, w_ref[:, 0:kw + vw])
            zf = _dot(ut, w_ref[:, kw + vw + d * kw:kw + vw + (d + 1) * kw])
            lb = lb_all[:, d * kw:(d + 1) * kw]
            e = jnp.exp(-jnp.abs(zf))
            log_sig = jnp.minimum(zf, 0.0) - jnp.log1p(e)
            a = jnp.log(lb)
            c = jnp.log1p(-lb) + log_sig
            log_f = jnp.maximum(a, c) + jnp.log1p(jnp.exp(-jnp.abs(a - c)))
            r = 1.0 / (1.0 + e)
            key = (1.0 - lb) * jnp.where(zf >= 0.0, e * r, r)
            g_hi, g_lo = _split2(log_f)
            b_s[d] = _dot(tri[d], g_hi) + _dot(tri[d], g_lo)
            q_s[d] = qv[:, 0:kw]
            v_s[d] = qv[:, kw:kw + vw]
            k_s[d] = key

        def chunk_step(cidx, carry2):
            for d in range(2):
                cl = cidx if d == 0 else n_chunk - 1 - cidx
                c0 = pl.multiple_of(cl * HG_CHUNK, HG_CHUNK)
                rows = pl.ds(c0, HG_CHUNK)
                g0 = pl.multiple_of(tile_of[d] * ROW_TILE + c0, HG_CHUNK)
                for g in range(n_grp):
                    lanes = slice(g * HG_GROUP, (g + 1) * HG_GROUP)
                    q = q_s[d, rows, lanes]
                    k = k_s[d, rows, lanes]
                    v = v_s[d, rows, lanes].astype(BF16)
                    b = b_s[d, rows, lanes]
                    b_ref = b_s[d, pl.ds(c0 + ref_row[d], 1), lanes]
                    b_end = b_s[d, pl.ds(c0 + end_row[d], 1), lanes]
                    qd = (q * jnp.exp(b - b_ref)).astype(BF16)
                    qs = (q * jnp.exp(b)).astype(BF16)
                    kd = (k * jnp.exp(b_ref - b)).astype(BF16)
                    kr = (k * jnp.exp(b_end - b)).astype(BF16)
                    kd_st = jnp.concatenate([kd] * heads_per_grp, axis=0) * head_mask
                    kr_st = jnp.concatenate([kr] * heads_per_grp, axis=0) * head_mask
                    v_st = jnp.concatenate([v] * heads_per_grp, axis=0) * head_mask
                    amat = jnp.where(causal[d], _dot_nt(qd, kd_st), 0.0).astype(BF16)
                    s_old = st_s[d * n_grp + g]
                    o = _dot(amat, v_st) + _dot_nt(qs, s_old.astype(BF16))
                    o_s[d, pl.ds(g0, HG_CHUNK), lanes] = o
                    st_s[d * n_grp + g] = s_old * jnp.exp(b_end) + _dot_tn(v_st, kr_st)
            return carry2

        lax.fori_loop(0, n_chunk, chunk_step, 0)
        return carry

    lax.fori_loop(0, n_tiles, tile_step, 0)

    hi_ = lax.broadcasted_iota(jnp.int32, (vw, vw), 0) // HG_DV
    hj_ = lax.broadcasted_iota(jnp.int32, (vw, vw), 1) // HG_DV
    head_sum = jnp.where(hi_ == hj_, 1.0, 0.0).astype(BF16)

    def finish(t, carry):
        r0 = pl.multiple_of(t * ROW_TILE, ROW_TILE)
        rows = pl.ds(r0, ROW_TILE)
        o = o_s[0, rows, :] + o_s[1, rows, :]
        sq_hi, sq_lo = _split2(o * o)
        ms = (_dot(sq_hi, head_sum) + _dot(sq_lo, head_sum)) * (1.0 / HG_DV)
        y = o * lax.rsqrt(ms + RMS_EPS) * gn_ref[...]
        gate = _silu(_dot(u_ref[0, rows, :], w_ref[:, 2 * kw + vw + kw:2 * kw + vw + kw + vw]))
        out_ref[0, rows, :] = (y * gate).astype(BF16)
        return carry

    lax.fori_loop(0, n_tiles, finish, 0)


def _hgrn(u, w_hg, lb, gn, n_lat):
    bsz, n, d = u.shape
    kw = HG_HEADS * HG_DK
    vw = HG_HEADS * HG_DV
    n_streams = 2 * (kw // HG_GROUP)
    est = (2 * n * d * 2 + 2 * w_hg.size * 2 + 2 * n * vw * 2
           + 2 * n * vw * 4 + n_streams * HG_GROUP * HG_GROUP * 4 + 8 * ROW_TILE * kw * 4
           + 6 * ROW_TILE * (2 * kw + vw) * 4)
    full = lambda a: pl.BlockSpec(a.shape, lambda b: (0,) * a.ndim)
    return pl.pallas_call(
        functools.partial(_hgrn_kernel, n_lat=n_lat),
        out_shape=jax.ShapeDtypeStruct((bsz, n, vw), BF16),
        grid=(bsz,),
        in_specs=[pl.BlockSpec((1, n, d), lambda b: (b, 0, 0)), full(w_hg), full(lb), full(gn)],
        out_specs=pl.BlockSpec((1, n, vw), lambda b: (b, 0, 0)),
        scratch_shapes=[
            pltpu.VMEM((2, n, vw), F32),
            pltpu.VMEM((n_streams, HG_GROUP, HG_GROUP), F32),
            pltpu.VMEM((2, ROW_TILE, kw), F32),
            pltpu.VMEM((2, ROW_TILE, vw), F32),
            pltpu.VMEM((2, ROW_TILE, kw), F32),
            pltpu.VMEM((2, ROW_TILE, kw), F32),
        ],
        compiler_params=pltpu.CompilerParams(
            dimension_semantics=("arbitrary",), vmem_limit_bytes=_vmem_limit(est)),
        name="hgrn_mixer",
    )(u, w_hg, lb, gn)


CV_PAD = 8


def _conv_kernel(u_ref, w_ref, cw_ref, cb_ref, out_ref, uu_s, coef_s, *, n_lat):
    n = u_ref.shape[1]
    cw = cw_ref.shape[-1]
    n_tiles = n // ROW_TILE
    lat_tiles = n_lat // ROW_TILE
    zero = jnp.zeros((CV_PAD, cw), F32)
    uu_s[0:CV_PAD, :] = zero
    uu_s[CV_PAD + n_lat:2 * CV_PAD + n_lat, :] = zero
    uu_s[2 * CV_PAD + n:3 * CV_PAD + n, :] = zero

    def off(t):
        return CV_PAD + t * ROW_TILE + (CV_PAD if t >= lat_tiles else 0)

    for t in range(n_tiles):
        rows = slice(t * ROW_TILE, (t + 1) * ROW_TILE)
        z = _dot(u_ref[0, rows, :], w_ref[...])
        uu_s[off(t):off(t) + ROW_TILE, :] = z[:, 2 * cw:3 * cw] * z[:, 0:cw]
        coef_s[rows, :] = z[:, cw:2 * cw] * _silu(z[:, 3 * cw:4 * cw])
    w0 = cw_ref[0:1, :]
    w1 = cw_ref[1:2, :]
    w2 = cw_ref[2:3, :]
    bias = cb_ref[...]
    for t in range(n_tiles):
        rows = slice(t * ROW_TILE, (t + 1) * ROW_TILE)
        o = off(t)
        conv = (uu_s[o - 1:o - 1 + ROW_TILE, :] * w0 + uu_s[o:o + ROW_TILE, :] * w1
                + uu_s[o + 1:o + 1 + ROW_TILE, :] * w2 + bias)
        out_ref[0, rows, :] = (coef_s[rows, :] * conv).astype(BF16)


def _conv(u, w_cv, conv_w, conv_b, n_lat):
    bsz, n, d = u.shape
    cw = conv_w.shape[-1]
    est = (2 * n * d * 2 + 2 * w_cv.size * 2 + 2 * n * cw * 2 + (n + 3 * CV_PAD) * cw * 4 + n * cw * 4
           + 4 * ROW_TILE * 4 * cw * 4)
    full = lambda a: pl.BlockSpec(a.shape, lambda b: (0,) * a.ndim)
    return pl.pallas_call(
        functools.partial(_conv_kernel, n_lat=n_lat),
        out_shape=jax.ShapeDtypeStruct((bsz, n, cw), BF16),
        grid=(bsz,),
        in_specs=[pl.BlockSpec((1, n, d), lambda b: (b, 0, 0)), full(w_cv), full(conv_w), full(conv_b)],
        out_specs=pl.BlockSpec((1, n, cw), lambda b: (b, 0, 0)),
        scratch_shapes=[pltpu.VMEM((n + 3 * CV_PAD, cw), F32), pltpu.VMEM((n, cw), F32)],
        compiler_params=pltpu.CompilerParams(
            dimension_semantics=("arbitrary",), vmem_limit_bytes=_vmem_limit(est)),
        name="conv_mixer",
    )(u, w_cv, conv_w, conv_b)


def _merge_kernel(u_ref, ym_ref, yh_ref, yc_ref, h_ref, modb_ref, modc_ref, wg_ref, wb_ref, wo_ref, fg_ref,
                  o_ref, *, n_lat, tm, last):
    d = h_ref.shape[-1]
    u = u_ref[0]
    acc = None
    for i, y_ref in enumerate((ym_ref, yh_ref, yc_ref)):
        gate_i = _sigmoid(_dot(u, wg_ref[:, i * d:(i + 1) * d]))
        term = gate_i * _dot(y_ref[0], wb_ref[i])
        acc = term if acc is None else acc + term
    out = _dot(acc.astype(BF16), wo_ref[...])
    row = pl.program_id(1) * tm + lax.broadcasted_iota(jnp.int32, (tm, 1), 0)
    gate = jnp.where(row < n_lat, modb_ref[0][:, 2 * d:3 * d], modc_ref[0][:, 2 * d:3 * d])
    hn = h_ref[0] + gate * out
    if last:
        hn = hn * lax.rsqrt(jnp.mean(hn * hn, axis=-1, keepdims=True) + RMS_EPS) * fg_ref[...]
    o_ref[0] = hn


def _merge(u, ym, yh, yc, h, mod3, w_gate, w_branch, w_out, final_g, n_lat, tm, last):
    bsz, n, d = h.shape
    bw = ym.shape[-1]
    n_out = n_lat if last else n
    ctx_row = bsz
    tok = lambda w: pl.BlockSpec((1, tm, w), lambda b, j: (b, j, 0))
    full = lambda a: pl.BlockSpec(a.shape, lambda b, j: (0,) * a.ndim)
    est = (2 * tm * (d * 2 + 3 * bw * 2 + 2 * d * 4) + 2 * (w_gate.size + w_branch.size + w_out.size) * 2
           + 6 * tm * d * 4)
    return pl.pallas_call(
        functools.partial(_merge_kernel, n_lat=n_lat, tm=tm, last=last),
        out_shape=jax.ShapeDtypeStruct((bsz, n_out, d), F32),
        grid=(bsz, n_out // tm),
        in_specs=[
            tok(d), tok(bw), tok(bw), tok(bw), tok(d),
            pl.BlockSpec((1, 1, 3 * d), lambda b, j: (b, 0, 0)),
            pl.BlockSpec((1, 1, 3 * d), lambda b, j: (ctx_row, 0, 0)),
            full(w_gate), full(w_branch), full(w_out), full(final_g),
        ],
        out_specs=tok(d),
        compiler_params=pltpu.CompilerParams(
            dimension_semantics=("arbitrary", "arbitrary"), vmem_limit_bytes=_vmem_limit(est)),
        name="merge_out",
    )(u, ym, yh, yc, h, mod3, mod3, w_gate, w_branch, w_out, final_g)


def _rope_tables(n_lat, n_ctx):
    pairs = MLA_ROPE // 4
    rows = n_lat // GRID_W
    row_id = np.repeat(np.arange(rows, dtype=np.float32), GRID_W)
    col_id = np.tile(np.arange(GRID_W, dtype=np.float32), rows)
    inv_freq = jnp.power(ROPE_BASE, -jnp.arange(pairs, dtype=F32) / pairs)
    ang = jnp.stack([row_id[:, None] * inv_freq, col_id[:, None] * inv_freq], axis=1)
    ang = jnp.broadcast_to(ang[:, :, None, :], (n_lat, 2, 2, pairs)).reshape(n_lat, MLA_ROPE)
    pad_hi = MLA_SLOT - MLA_NOPE - MLA_ROPE
    cos = jnp.concatenate([jnp.ones((n_lat, MLA_NOPE), F32), jnp.cos(ang), jnp.ones((n_lat, pad_hi), F32)], axis=1)
    sin = jnp.concatenate([jnp.zeros((n_lat, MLA_NOPE), F32), jnp.sin(ang), jnp.zeros((n_lat, pad_hi), F32)], axis=1)
    cos = jnp.concatenate([cos, jnp.ones((n_ctx, MLA_SLOT), F32)], axis=0)
    sin = jnp.concatenate([sin, jnp.zeros((n_ctx, MLA_SLOT), F32)], axis=0)
    return cos, sin


def _rotate_half_cols(w):
    pairs = MLA_ROPE // 4
    ws = w.reshape(w.shape[:-1] + (2, 2, pairs))
    return jnp.stack([-ws[..., 1, :], ws[..., 0, :]], axis=-2).reshape(w.shape)


def _mla_weights(w_in_l, w_uq_l, w_ukv_l, q_rank, kv_rank):
    d = w_in_l.shape[0]
    width = MLA_HEADS * MLA_V
    pad_hi = MLA_SLOT - MLA_NOPE - MLA_ROPE
    c_kr = q_rank + kv_rank
    w_kr = w_in_l[:, c_kr:c_kr + MLA_ROPE]
    slot = lambda w: jnp.concatenate([jnp.zeros((d, MLA_NOPE), w.dtype), w, jnp.zeros((d, pad_hi), w.dtype)], axis=1)
    w_mla = jnp.concatenate([w_in_l[:, 0:c_kr], slot(w_kr), slot(_rotate_half_cols(w_kr)),
                             w_in_l[:, c_kr + MLA_ROPE:c_kr + MLA_ROPE + width]], axis=1)
    uq = w_uq_l.reshape(q_rank, MLA_HEADS, MLA_NOPE + MLA_ROPE)
    uq_n, uq_r = uq[..., :MLA_NOPE], uq[..., MLA_NOPE:]
    zn = jnp.zeros_like(uq_n)
    zp = jnp.zeros((q_rank, MLA_HEADS, pad_hi), uq.dtype)
    main = jnp.concatenate([uq_n, uq_r, zp], axis=-1).reshape(q_rank, MLA_HEADS * MLA_SLOT)
    part = jnp.concatenate([zn, _rotate_half_cols(uq_r), zp], axis=-1).reshape(q_rank, MLA_HEADS * MLA_SLOT)
    wuq2 = jnp.concatenate([main, part], axis=1)
    ukv = w_ukv_l.reshape(kv_rank, MLA_HEADS, MLA_NOPE + MLA_V)
    kn = ukv[..., :MLA_NOPE]
    wukn = jnp.concatenate([kn, jnp.zeros((kv_rank, MLA_HEADS, MLA_SLOT - MLA_NOPE), kn.dtype)], axis=-1)
    wukn = wukn.reshape(kv_rank, MLA_HEADS * MLA_SLOT)
    wvt = ukv[..., MLA_NOPE:].reshape(kv_rank, width).T
    return w_mla, wuq2, wukn, wvt


def kernel(x, c, ctx, c_ctx, ada_w, ada_b, norm_g, w_in, mla_q_norm_g, mla_kv_norm_g, mla_w_uq, mla_w_ukv,
           hg_lb_logits, hg_norm_g, conv_w, conv_b, w_branch, w_out, final_norm_g):
    bsz, n_lat, d = x.shape
    n_ctx = ctx.shape[1]
    n = n_lat + n_ctx
    depth = w_in.shape[0]
    q_rank = mla_q_norm_g.shape[-1]
    kv_rank = mla_kv_norm_g.shape[-1]
    width = MLA_HEADS * MLA_V
    kw = HG_HEADS * HG_DK
    vw = HG_HEADS * HG_DV
    cw = conv_w.shape[-1]
    assert n_lat % ROW_TILE == 0 and n_ctx % ROW_TILE == 0 and n_lat % GRID_W == 0
    assert width == vw == cw == w_branch.shape[2]

    pad = (-(bsz + 1)) % 8
    cs = jnp.concatenate([c, c_ctx[None, :], jnp.zeros((pad, d), c.dtype)], axis=0).astype(F32)
    mod = _modulation(cs, ada_w.astype(F32), ada_b.astype(F32))
    lb_all = _lower_bounds(hg_lb_logits)
    cos_t, sin_t = _rope_tables(n_lat, n_ctx)

    c_hg = q_rank + kv_rank + MLA_ROPE + width
    c_cv = c_hg + 2 * kw + vw + kw + vw
    c_gate = c_cv + 4 * cw
    w_in_b = w_in.astype(BF16)
    w_branch_b = w_branch.astype(BF16)
    w_out_b = w_out.astype(BF16)
    final_g = final_norm_g.reshape(1, d).astype(F32)

    h = jnp.concatenate([x, ctx], axis=1).astype(F32)
    tm_norm = 256
    tm_merge = 384 if n % 384 == 0 else ROW_TILE
    tm_last = 512 if n_lat % 512 == 0 else ROW_TILE
    for l in range(depth):
        last = l == depth - 1
        mod3 = mod[l].reshape(mod.shape[1], 1, 3 * d)
        u = _norm_mod(h, mod3, norm_g[l].astype(F32), n_lat, tm_norm)
        w_mla, wuq2, wukn, wvt = _mla_weights(w_in_b[l], mla_w_uq[l].astype(BF16), mla_w_ukv[l].astype(BF16),
                                              q_rank, kv_rank)
        y_mla = _mla(u, w_mla, mla_q_norm_g[l].reshape(1, q_rank).astype(F32),
                     mla_kv_norm_g[l].reshape(1, kv_rank).astype(F32), wuq2, wukn, wvt, cos_t, sin_t, n_lat)
        y_hg = _hgrn(u, w_in_b[l, :, c_hg:c_cv], lb_all[l].reshape(1, 2 * kw),
                     hg_norm_g[l].reshape(1, vw).astype(F32), n_lat)
        y_cv = _conv(u, w_in_b[l, :, c_cv:c_gate], conv_w[l].astype(F32), conv_b[l].reshape(1, cw).astype(F32), n_lat)
        h = _merge(u, y_mla, y_hg, y_cv, h, mod3, w_in_b[l, :, c_gate:], w_branch_b[l], w_out_b[l], final_g,
                   n_lat, tm_last if last else tm_merge, last)
    return h
```

```python
import functools

import numpy as np
import jax
import jax.numpy as jnp
from jax import lax
from jax.experimental import pallas as pl
from jax.experimental.pallas import tpu as pltpu

F32 = jnp.float32
BF16 = jnp.bfloat16

RMS_EPS = 1e-6
GRID_W = 64
ROPE_BASE = 10000.0
MLA_HEADS = 8
MLA_NOPE = 64
MLA_ROPE = 32
MLA_V = 64
MLA_SLOT = 128
MLA_SCALE = (MLA_NOPE + MLA_ROPE) ** -0.5
MLA_SCALE_LOG2 = MLA_SCALE * float(np.log2(np.e))
HG_HEADS = 8
HG_DK = 64
HG_DV = 64
HG_GROUP = 256
HG_CHUNK = 32
CV_K = 3

V7X_VMEM_BYTES = 64 * 1024 * 1024
ROW_TILE = 256


def _vmem_limit(nbytes):
    return int(min(nbytes + (12 << 20), V7X_VMEM_BYTES - (6 << 20)))


def _dot(a, b):
    return jnp.dot(a, b, preferred_element_type=F32)


def _dot_nt(a, b):
    return lax.dot_general(a, b, (((1,), (1,)), ((), ())), preferred_element_type=F32)


def _dot_tn(a, b):
    return lax.dot_general(a, b, (((0,), (0,)), ((), ())), preferred_element_type=F32)


def _sigmoid(x):
    return 1.0 / (1.0 + jnp.exp(-x))


def _silu(x):
    return x * _sigmoid(x)


def _split2(x):
    hi = x.astype(BF16)
    lo = (x - hi.astype(F32)).astype(BF16)
    return hi, lo


def _mod_kernel(c_ref, w_ref, b_ref, o_ref):
    s = _silu(c_ref[...])
    o_ref[0] = jnp.dot(s, w_ref[0], preferred_element_type=F32, precision=lax.Precision.HIGHEST) + b_ref[0]


def _modulation(cs, ada_w, ada_b):
    depth, d, d3 = ada_w.shape
    rows = cs.shape[0]
    nb = d3 // d
    return pl.pallas_call(
        _mod_kernel,
        out_shape=jax.ShapeDtypeStruct((depth, rows, d3), F32),
        grid=(depth, nb),
        in_specs=[
            pl.BlockSpec((rows, d), lambda l, j: (0, 0)),
            pl.BlockSpec((1, d, d), lambda l, j: (l, 0, j)),
            pl.BlockSpec((1, 1, d), lambda l, j: (l, 0, j)),
        ],
        out_specs=pl.BlockSpec((1, rows, d), lambda l, j: (l, 0, j)),
        compiler_params=pltpu.CompilerParams(
            dimension_semantics=("arbitrary", "arbitrary"),
            vmem_limit_bytes=_vmem_limit(2 * d * d * 4 + 4 * rows * d * 4),
        ),
        name="adaln_mod",
    )(cs, ada_w, ada_b.reshape(depth, 1, d3))


def _lb_kernel(x_ref, o_ref):
    x = x_ref[...]
    m = jnp.max(x, axis=0, keepdims=True)
    e = jnp.exp(x - m)
    p = e / jnp.sum(e, axis=0, keepdims=True)
    depth = x.shape[0]
    rows = [p[0:1]]
    for l in range(1, depth):
        rows.append(rows[-1] + p[l:l + 1])
    cs = jnp.concatenate(rows, axis=0)
    o_ref[...] = cs - cs[0:1]


def _lower_bounds(hg_lb_logits):
    depth = hg_lb_logits.shape[0]
    flat = hg_lb_logits.reshape(depth, -1).astype(F32)
    return pl.pallas_call(
        _lb_kernel,
        out_shape=jax.ShapeDtypeStruct(flat.shape, F32),
        name="hgrn_lower_bounds",
    )(flat)


def _norm_kernel(h_ref, modb_ref, modc_ref, g_ref, u_ref, *, n_lat, tm):
    d = h_ref.shape[-1]
    x = h_ref[0]
    ms = jnp.mean(x * x, axis=-1, keepdims=True)
    y = x * lax.rsqrt(ms + RMS_EPS) * g_ref[...]
    row = pl.program_id(1) * tm + lax.broadcasted_iota(jnp.int32, (tm, 1), 0)
    is_lat = row < n_lat
    mb = modb_ref[0]
    mc = modc_ref[0]
    shift = jnp.where(is_lat, mb[:, 0:d], mc[:, 0:d])
    scale = jnp.where(is_lat, mb[:, d:2 * d], mc[:, d:2 * d])
    u_ref[0] = (y * (1.0 + scale) + shift).astype(BF16)


def _norm_mod(h, mod3, norm_g, n_lat, tm):
    bsz, n, d = h.shape
    ctx_row = bsz
    return pl.pallas_call(
        functools.partial(_norm_kernel, n_lat=n_lat, tm=tm),
        out_shape=jax.ShapeDtypeStruct((bsz, n, d), BF16),
        grid=(bsz, n // tm),
        in_specs=[
            pl.BlockSpec((1, tm, d), lambda b, j: (b, j, 0)),
            pl.BlockSpec((1, 1, 3 * d), lambda b, j: (b, 0, 0)),
            pl.BlockSpec((1, 1, 3 * d), lambda b, j: (ctx_row, 0, 0)),
            pl.BlockSpec((1, d), lambda b, j: (0, 0)),
        ],
        out_specs=pl.BlockSpec((1, tm, d), lambda b, j: (b, j, 0)),
        compiler_params=pltpu.CompilerParams(
            dimension_semantics=("arbitrary", "arbitrary"),
            vmem_limit_bytes=_vmem_limit(2 * tm * d * 6 + 4 * tm * d * 4),
        ),
        name="norm_mod",
    )(h, mod3, mod3, norm_g.reshape(1, d))


def _mla_kernel(u_ref, w_ref, gq_ref, gkv_ref, wuq_ref, wukn_ref, wvt_ref, cos_ref, sin_ref, out_ref,
                q_s, k_s, vt_s, g_s, ot_s, st_s, *, n_lat, q_rank, kv_rank):
    n = u_ref.shape[1]
    n_ctx = n - n_lat
    hs = MLA_HEADS * MLA_SLOT
    width = MLA_HEADS * MLA_V
    c_kv = q_rank
    c_kr = q_rank + kv_rank
    c_krs = c_kr + MLA_SLOT
    c_g = c_krs + MLA_SLOT

    for t in range(n // ROW_TILE):
        rows = slice(t * ROW_TILE, (t + 1) * ROW_TILE)
        z = _dot(u_ref[0, rows, :], w_ref[...])
        cos_t = cos_ref[rows, :]
        sin_t = sin_ref[rows, :]
        cos8 = jnp.concatenate([cos_t] * MLA_HEADS, axis=1)
        sin8 = jnp.concatenate([sin_t] * MLA_HEADS, axis=1)
        cq = z[:, 0:q_rank]
        cqn = (cq * lax.rsqrt(jnp.mean(cq * cq, axis=-1, keepdims=True) + RMS_EPS) * gq_ref[...]).astype(BF16)
        q2 = _dot(cqn, wuq_ref[...])
        q = ((q2[:, 0:hs] * cos8 + q2[:, hs:2 * hs] * sin8) * MLA_SCALE_LOG2).astype(BF16)
        ckv = z[:, c_kv:c_kr]
        kvn = (ckv * lax.rsqrt(jnp.mean(ckv * ckv, axis=-1, keepdims=True) + RMS_EPS) * gkv_ref[...]).astype(BF16)
        kn = _dot(kvn, wukn_ref[...])
        kr = z[:, c_kr:c_krs] * cos_t + z[:, c_krs:c_g] * sin_t
        k = (kn + jnp.concatenate([kr] * MLA_HEADS, axis=1)).astype(BF16)
        for h in range(MLA_HEADS):
            q_s[h, rows, :] = q[:, h * MLA_SLOT:(h + 1) * MLA_SLOT]
            k_s[h, rows, :] = k[:, h * MLA_SLOT:(h + 1) * MLA_SLOT]
        vt = _dot_nt(wvt_ref[...], kvn).astype(BF16)
        for h in range(MLA_HEADS):
            vt_s[h, :, rows] = vt[h * MLA_V:(h + 1) * MLA_V, :]
        g_s[rows, :] = _silu(z[:, c_g:c_g + width])

    def attend(r0, k0, nk):
        rows = pl.ds(r0, ROW_TILE)

        def scores(h):
            st = _dot_nt(k_s[h, k0:k0 + nk, :], q_s[h, rows, :])
            st_s[h % 2, 0:nk, :] = st
            return jnp.max(st, axis=0, keepdims=True)

        m_next = scores(0)
        for h in range(MLA_HEADS):
            m = m_next
            if h + 1 < MLA_HEADS:
                m_next = scores(h + 1)
            p = jnp.exp2(st_s[h % 2, 0:nk, :] - m)
            l = jnp.sum(p, axis=0, keepdims=True)
            ot = _dot(vt_s[h, :, k0:k0 + nk], p.astype(BF16))
            ot_s[h] = ot * (1.0 / l)
        o = jnp.concatenate([ot_s[h] for h in range(MLA_HEADS)], axis=0).T
        out_ref[0, rows, :] = (o * g_s[rows, :]).astype(BF16)

    def lat_tile(t, carry):
        attend(pl.multiple_of(t * ROW_TILE, ROW_TILE), 0, n)
        return carry

    lax.fori_loop(0, n_lat // ROW_TILE, lat_tile, 0)
    for t in range(n_ctx // ROW_TILE):
        attend(n_lat + t * ROW_TILE, n_lat, n_ctx)


def _mla(u, w_mla, gq, gkv, wuq2, wukn, wvt, cos_t, sin_t, n_lat):
    bsz, n, d = u.shape
    q_rank, kv_rank = gq.shape[-1], gkv.shape[-1]
    width = MLA_HEADS * MLA_V
    hs = MLA_HEADS * MLA_SLOT
    est = (2 * n * d * 2 + 2 * w_mla.size * 2 + 2 * (wuq2.size + wukn.size + wvt.size) * 2
           + 4 * n * MLA_SLOT * 4 + 2 * n * width * 2
           + 2 * n * hs * 2 + n * width * 2 + n * width * 4 + width * ROW_TILE * 4
           + 2 * n * ROW_TILE * 4 + 2 * n * ROW_TILE * 4)
    full = lambda a: pl.BlockSpec(a.shape, lambda b: (0,) * a.ndim)
    return pl.pallas_call(
        functools.partial(_mla_kernel, n_lat=n_lat, q_rank=q_rank, kv_rank=kv_rank),
        out_shape=jax.ShapeDtypeStruct((bsz, n, width), BF16),
        grid=(bsz,),
        in_specs=[
            pl.BlockSpec((1, n, d), lambda b: (b, 0, 0)),
            full(w_mla), full(gq), full(gkv), full(wuq2), full(wukn), full(wvt), full(cos_t), full(sin_t),
        ],
        out_specs=pl.BlockSpec((1, n, width), lambda b: (b, 0, 0)),
        scratch_shapes=[
            pltpu.VMEM((MLA_HEADS, n, MLA_SLOT), BF16),
            pltpu.VMEM((MLA_HEADS, n, MLA_SLOT), BF16),
            pltpu.VMEM((MLA_HEADS, MLA_V, n), BF16),
            pltpu.VMEM((n, width), F32),
            pltpu.VMEM((MLA_HEADS, MLA_V, ROW_TILE), F32),
            pltpu.VMEM((2, n, ROW_TILE), F32),
        ],
        compiler_params=pltpu.CompilerParams(
            dimension_semantics=("arbitrary",), vmem_limit_bytes=_vmem_limit(est)),
        name="mla_mixer",
    )(u, w_mla, gq, gkv, wuq2, wukn, wvt, cos_t, sin_t)


def _hgrn_kernel(u_ref, w_ref, lb_ref, gn_ref, out_ref,
                 o_s, st_s, q_s, v_s, k_s, b_s, *, n_lat):
    n = u_ref.shape[1]
    kw = HG_HEADS * HG_DK
    vw = HG_HEADS * HG_DV
    n_tiles = n // ROW_TILE
    lat_tiles = n_lat // ROW_TILE
    ctx_tiles = n_tiles - lat_tiles
    n_chunk = ROW_TILE // HG_CHUNK
    n_grp = kw // HG_GROUP
    heads_per_grp = HG_GROUP // HG_DK
    stack = heads_per_grp * HG_CHUNK

    ti = lax.broadcasted_iota(jnp.int32, (ROW_TILE, ROW_TILE), 0)
    si = lax.broadcasted_iota(jnp.int32, (ROW_TILE, ROW_TILE), 1)
    same = (ti // HG_CHUNK) == (si // HG_CHUNK)
    tri = (jnp.where(same & (si <= ti), 1.0, 0.0).astype(BF16),
           jnp.where(same & (si >= ti), 1.0, 0.0).astype(BF16))
    hr = lax.broadcasted_iota(jnp.int32, (stack, HG_GROUP), 0) // HG_CHUNK
    hc = lax.broadcasted_iota(jnp.int32, (stack, HG_GROUP), 1) // HG_DK
    head_mask = jnp.where(hr == hc, 1.0, 0.0).astype(BF16)
    at = lax.broadcasted_iota(jnp.int32, (HG_CHUNK, stack), 0)
    as_ = lax.broadcasted_iota(jnp.int32, (HG_CHUNK, stack), 1) % HG_CHUNK
    causal = (as_ <= at, as_ >= at)
    ref_row = (HG_CHUNK // 2 - 1, HG_CHUNK // 2)
    end_row = (HG_CHUNK - 1, 0)

    lb_all = lb_ref[...]

    st_s[...] = jnp.zeros_like(st_s)

    def tile_step(i, carry):
        t_fwd = jnp.where(i < ctx_tiles, lat_tiles + i, i - ctx_tiles)
        t_bwd = n_tiles - 1 - i
        tile_of = (t_fwd, t_bwd)
        for d in range(2):
            r0 = pl.multiple_of(tile_of[d] * ROW_TILE, ROW_TILE)
            ut = u_ref[0, pl.ds(r0, ROW_TILE), :]
            qv = _dot(ut, w_ref[:, 0:kw + vw])
            zf = _dot(ut, w_ref[:, kw + vw + d * kw:kw + vw + (d + 1) * kw])
            lb = lb_all[:, d * kw:(d + 1) * kw]
            e = jnp.exp(-jnp.abs(zf))
            log_sig = jnp.minimum(zf, 0.0) - jnp.log1p(e)
            a = jnp.log(lb)
            c = jnp.log1p(-lb) + log_sig
            log_f = jnp.maximum(a, c) + jnp.log1p(jnp.exp(-jnp.abs(a - c)))
            r = 1.0 / (1.0 + e)
            key = (1.0 - lb) * jnp.where(zf >= 0.0, e * r, r)
            g_hi, g_lo = _split2(log_f)
            b_s[d] = _dot(tri[d], g_hi) + _dot(tri[d], g_lo)
            q_s[d] = qv[:, 0:kw]
            v_s[d] = qv[:, kw:kw + vw]
            k_s[d] = key

        for cidx in range(n_chunk):
            for d in range(2):
                cl = cidx if d == 0 else n_chunk - 1 - cidx
                c0 = cl * HG_CHUNK
                rows = slice(c0, c0 + HG_CHUNK)
                g0 = pl.multiple_of(tile_of[d] * ROW_TILE + c0, HG_CHUNK)
                for g in range(n_grp):
                    lanes = slice(g * HG_GROUP, (g + 1) * HG_GROUP)
                    q = q_s[d, rows, lanes]
                    k = k_s[d, rows, lanes]
                    v = v_s[d, rows, lanes].astype(BF16)
                    b = b_s[d, rows, lanes]
                    b_ref = b_s[d, c0 + ref_row[d]:c0 + ref_row[d] + 1, lanes]
                    b_end = b_s[d, c0 + end_row[d]:c0 + end_row[d] + 1, lanes]
                    qd = (q * jnp.exp(b - b_ref)).astype(BF16)
                    qs = (q * jnp.exp(b)).astype(BF16)
                    kd = (k * jnp.exp(b_ref - b)).astype(BF16)
                    kr = (k * jnp.exp(b_end - b)).astype(BF16)
                    kd_st = jnp.concatenate([kd] * heads_per_grp, axis=0) * head_mask
                    kr_st = jnp.concatenate([kr] * heads_per_grp, axis=0) * head_mask
                    v_st = jnp.concatenate([v] * heads_per_grp, axis=0) * head_mask
                    amat = jnp.where(causal[d], _dot_nt(qd, kd_st), 0.0).astype(BF16)
                    s_old = st_s[d * n_grp + g]
                    o = _dot(amat, v_st) + _dot_nt(qs, s_old.astype(BF16))
                    o_s[d, pl.ds(g0, HG_CHUNK), lanes] = o
                    st_s[d * n_grp + g] = s_old * jnp.exp(b_end) + _dot_tn(v_st, kr_st)
        return carry

    lax.fori_loop(0, n_tiles, tile_step, 0)

    hi_ = lax.broadcasted_iota(jnp.int32, (vw, vw), 0) // HG_DV
    hj_ = lax.broadcasted_iota(jnp.int32, (vw, vw), 1) // HG_DV
    head_sum = jnp.where(hi_ == hj_, 1.0, 0.0).astype(BF16)

    def finish(t, carry):
        r0 = pl.multiple_of(t * ROW_TILE, ROW_TILE)
        rows = pl.ds(r0, ROW_TILE)
        o = o_s[0, rows, :] + o_s[1, rows, :]
        sq_hi, sq_lo = _split2(o * o)
        ms = (_dot(sq_hi, head_sum) + _dot(sq_lo, head_sum)) * (1.0 / HG_DV)
        y = o * lax.rsqrt(ms + RMS_EPS) * gn_ref[...]
        gate = _silu(_dot(u_ref[0, rows, :], w_ref[:, 2 * kw + vw + kw:2 * kw + vw + kw + vw]))
        out_ref[0, rows, :] = (y * gate).astype(BF16)
        return carry

    lax.fori_loop(0, n_tiles, finish, 0)


def _hgrn(u, w_hg, lb, gn, n_lat):
    bsz, n, d = u.shape
    kw = HG_HEADS * HG_DK
    vw = HG_HEADS * HG_DV
    n_streams = 2 * (kw // HG_GROUP)
    est = (2 * n * d * 2 + 2 * w_hg.size * 2 + 2 * n * vw * 2
           + 2 * n * vw * 4 + n_streams * HG_GROUP * HG_GROUP * 4 + 8 * ROW_TILE * kw * 4
           + 6 * ROW_TILE * (2 * kw + vw) * 4)
    full = lambda a: pl.BlockSpec(a.shape, lambda b: (0,) * a.ndim)
    return pl.pallas_call(
        functools.partial(_hgrn_kernel, n_lat=n_lat),
        out_shape=jax.ShapeDtypeStruct((bsz, n, vw), BF16),
        grid=(bsz,),
        in_specs=[pl.BlockSpec((1, n, d), lambda b: (b, 0, 0)), full(w_hg), full(lb), full(gn)],
        out_specs=pl.BlockSpec((1, n, vw), lambda b: (b, 0, 0)),
        scratch_shapes=[
            pltpu.VMEM((2, n, vw), F32),
            pltpu.VMEM((n_streams, HG_GROUP, HG_GROUP), F32),
            pltpu.VMEM((2, ROW_TILE, kw), F32),
            pltpu.VMEM((2, ROW_TILE, vw), F32),
            pltpu.VMEM((2, ROW_TILE, kw), F32),
            pltpu.VMEM((2, ROW_TILE, kw), F32),
        ],
        compiler_params=pltpu.CompilerParams(
            dimension_semantics=("arbitrary",), vmem_limit_bytes=_vmem_limit(est)),
        name="hgrn_mixer",
    )(u, w_hg, lb, gn)


CV_PAD = 8


def _conv_kernel(u_ref, w_ref, cw_ref, cb_ref, out_ref, uu_s, coef_s, *, n_lat):
    n = u_ref.shape[1]
    cw = cw_ref.shape[-1]
    n_tiles = n // ROW_TILE
    lat_tiles = n_lat // ROW_TILE
    zero = jnp.zeros((CV_PAD, cw), F32)
    uu_s[0:CV_PAD, :] = zero
    uu_s[CV_PAD + n_lat:2 * CV_PAD + n_lat, :] = zero
    uu_s[2 * CV_PAD + n:3 * CV_PAD + n, :] = zero

    def off(t):
        return CV_PAD + t * ROW_TILE + (CV_PAD if t >= lat_tiles else 0)

    for t in range(n_tiles):
        rows = slice(t * ROW_TILE, (t + 1) * ROW_TILE)
        z = _dot(u_ref[0, rows, :], w_ref[...])
        uu_s[off(t):off(t) + ROW_TILE, :] = z[:, 2 * cw:3 * cw] * z[:, 0:cw]
        coef_s[rows, :] = z[:, cw:2 * cw] * _silu(z[:, 3 * cw:4 * cw])
    w0 = cw_ref[0:1, :]
    w1 = cw_ref[1:2, :]
    w2 = cw_ref[2:3, :]
    bias = cb_ref[...]
    for t in range(n_tiles):
        rows = slice(t * ROW_TILE, (t + 1) * ROW_TILE)
        o = off(t)
        conv = (uu_s[o - 1:o - 1 + ROW_TILE, :] * w0 + uu_s[o:o + ROW_TILE, :] * w1
                + uu_s[o + 1:o + 1 + ROW_TILE, :] * w2 + bias)
        out_ref[0, rows, :] = (coef_s[rows, :] * conv).astype(BF16)


def _conv(u, w_cv, conv_w, conv_b, n_lat):
    bsz, n, d = u.shape
    cw = conv_w.shape[-1]
    est = (2 * n * d * 2 + 2 * w_cv.size * 2 + 2 * n * cw * 2 + (n + 3 * CV_PAD) * cw * 4 + n * cw * 4
           + 4 * ROW_TILE * 4 * cw * 4)
    full = lambda a: pl.BlockSpec(a.shape, lambda b: (0,) * a.ndim)
    return pl.pallas_call(
        functools.partial(_conv_kernel, n_lat=n_lat),
        out_shape=jax.ShapeDtypeStruct((bsz, n, cw), BF16),
        grid=(bsz,),
        in_specs=[pl.BlockSpec((1, n, d), lambda b: (b, 0, 0)), full(w_cv), full(conv_w), full(conv_b)],
        out_specs=pl.BlockSpec((1, n, cw), lambda b: (b, 0, 0)),
        scratch_shapes=[pltpu.VMEM((n + 3 * CV_PAD, cw), F32), pltpu.VMEM((n, cw), F32)],
        compiler_params=pltpu.CompilerParams(
            dimension_semantics=("arbitrary",), vmem_limit_bytes=_vmem_limit(est)),
        name="conv_mixer",
    )(u, w_cv, conv_w, conv_b)


def _merge_kernel(u_ref, ym_ref, yh_ref, yc_ref, h_ref, modb_ref, modc_ref, wg_ref, wb_ref, wo_ref, fg_ref,
                  o_ref, *, n_lat, tm, last):
    d = h_ref.shape[-1]
    u = u_ref[0]
    acc = None
    for i, y_ref in enumerate((ym_ref, yh_ref, yc_ref)):
        gate_i = _sigmoid(_dot(u, wg_ref[:, i * d:(i + 1) * d]))
        term = gate_i * _dot(y_ref[0], wb_ref[i])
        acc = term if acc is None else acc + term
    out = _dot(acc.astype(BF16), wo_ref[...])
    row = pl.program_id(1) * tm + lax.broadcasted_iota(jnp.int32, (tm, 1), 0)
    gate = jnp.where(row < n_lat, modb_ref[0][:, 2 * d:3 * d], modc_ref[0][:, 2 * d:3 * d])
    hn = h_ref[0] + gate * out
    if last:
        hn = hn * lax.rsqrt(jnp.mean(hn * hn, axis=-1, keepdims=True) + RMS_EPS) * fg_ref[...]
    o_ref[0] = hn


def _merge(u, ym, yh, yc, h, mod3, w_gate, w_branch, w_out, final_g, n_lat, tm, last):
    bsz, n, d = h.shape
    bw = ym.shape[-1]
    n_out = n_lat if last else n
    ctx_row = bsz
    tok = lambda w: pl.BlockSpec((1, tm, w), lambda b, j: (b, j, 0))
    full = lambda a: pl.BlockSpec(a.shape, lambda b, j: (0,) * a.ndim)
    est = (2 * tm * (d * 2 + 3 * bw * 2 + 2 * d * 4) + 2 * (w_gate.size + w_branch.size + w_out.size) * 2
           + 6 * tm * d * 4)
    return pl.pallas_call(
        functools.partial(_merge_kernel, n_lat=n_lat, tm=tm, last=last),
        out_shape=jax.ShapeDtypeStruct((bsz, n_out, d), F32),
        grid=(bsz, n_out // tm),
        in_specs=[
            tok(d), tok(bw), tok(bw), tok(bw), tok(d),
            pl.BlockSpec((1, 1, 3 * d), lambda b, j: (b, 0, 0)),
            pl.BlockSpec((1, 1, 3 * d), lambda b, j: (ctx_row, 0, 0)),
            full(w_gate), full(w_branch), full(w_out), full(final_g),
        ],
        out_specs=tok(d),
        compiler_params=pltpu.CompilerParams(
            dimension_semantics=("arbitrary", "arbitrary"), vmem_limit_bytes=_vmem_limit(est)),
        name="merge_out",
    )(u, ym, yh, yc, h, mod3, mod3, w_gate, w_branch, w_out, final_g)


def _rope_tables(n_lat, n_ctx):
    pairs = MLA_ROPE // 4
    rows = n_lat // GRID_W
    row_id = np.repeat(np.arange(rows, dtype=np.float32), GRID_W)
    col_id = np.tile(np.arange(GRID_W, dtype=np.float32), rows)
    inv_freq = jnp.power(ROPE_BASE, -jnp.arange(pairs, dtype=F32) / pairs)
    ang = jnp.stack([row_id[:, None] * inv_freq, col_id[:, None] * inv_freq], axis=1)
    ang = jnp.broadcast_to(ang[:, :, None, :], (n_lat, 2, 2, pairs)).reshape(n_lat, MLA_ROPE)
    pad_hi = MLA_SLOT - MLA_NOPE - MLA_ROPE
    cos = jnp.concatenate([jnp.ones((n_lat, MLA_NOPE), F32), jnp.cos(ang), jnp.ones((n_lat, pad_hi), F32)], axis=1)
    sin = jnp.concatenate([jnp.zeros((n_lat, MLA_NOPE), F32), jnp.sin(ang), jnp.zeros((n_lat, pad_hi), F32)], axis=1)
    cos = jnp.concatenate([cos, jnp.ones((n_ctx, MLA_SLOT), F32)], axis=0)
    sin = jnp.concatenate([sin, jnp.zeros((n_ctx, MLA_SLOT), F32)], axis=0)
    return cos, sin


def _rotate_half_cols(w):
    pairs = MLA_ROPE // 4
    ws = w.reshape(w.shape[:-1] + (2, 2, pairs))
    return jnp.stack([-ws[..., 1, :], ws[..., 0, :]], axis=-2).reshape(w.shape)


def _mla_weights(w_in_l, w_uq_l, w_ukv_l, q_rank, kv_rank):
    d = w_in_l.shape[0]
    width = MLA_HEADS * MLA_V
    pad_hi = MLA_SLOT - MLA_NOPE - MLA_ROPE
    c_kr = q_rank + kv_rank
    w_kr = w_in_l[:, c_kr:c_kr + MLA_ROPE]
    slot = lambda w: jnp.concatenate([jnp.zeros((d, MLA_NOPE), w.dtype), w, jnp.zeros((d, pad_hi), w.dtype)], axis=1)
    w_mla = jnp.concatenate([w_in_l[:, 0:c_kr], slot(w_kr), slot(_rotate_half_cols(w_kr)),
                             w_in_l[:, c_kr + MLA_ROPE:c_kr + MLA_ROPE + width]], axis=1)
    uq = w_uq_l.reshape(q_rank, MLA_HEADS, MLA_NOPE + MLA_ROPE)
    uq_n, uq_r = uq[..., :MLA_NOPE], uq[..., MLA_NOPE:]
    zn = jnp.zeros_like(uq_n)
    zp = jnp.zeros((q_rank, MLA_HEADS, pad_hi), uq.dtype)
    main = jnp.concatenate([uq_n, uq_r, zp], axis=-1).reshape(q_rank, MLA_HEADS * MLA_SLOT)
    part = jnp.concatenate([zn, _rotate_half_cols(uq_r), zp], axis=-1).reshape(q_rank, MLA_HEADS * MLA_SLOT)
    wuq2 = jnp.concatenate([main, part], axis=1)
    ukv = w_ukv_l.reshape(kv_rank, MLA_HEADS, MLA_NOPE + MLA_V)
    kn = ukv[..., :MLA_NOPE]
    wukn = jnp.concatenate([kn, jnp.zeros((kv_rank, MLA_HEADS, MLA_SLOT - MLA_NOPE), kn.dtype)], axis=-1)
    wukn = wukn.reshape(kv_rank, MLA_HEADS * MLA_SLOT)
    wvt = ukv[..., MLA_NOPE:].reshape(kv_rank, width).T
    return w_mla, wuq2, wukn, wvt


def kernel(x, c, ctx, c_ctx, ada_w, ada_b, norm_g, w_in, mla_q_norm_g, mla_kv_norm_g, mla_w_uq, mla_w_ukv,
           hg_lb_logits, hg_norm_g, conv_w, conv_b, w_branch, w_out, final_norm_g):
    bsz, n_lat, d = x.shape
    n_ctx = ctx.shape[1]
    n = n_lat + n_ctx
    depth = w_in.shape[0]
    q_rank = mla_q_norm_g.shape[-1]
    kv_rank = mla_kv_norm_g.shape[-1]
    width = MLA_HEADS * MLA_V
    kw = HG_HEADS * HG_DK
    vw = HG_HEADS * HG_DV
    cw = conv_w.shape[-1]
    assert n_lat % ROW_TILE == 0 and n_ctx % ROW_TILE == 0 and n_lat % GRID_W == 0
    assert width == vw == cw == w_branch.shape[2]

    pad = (-(bsz + 1)) % 8
    cs = jnp.concatenate([c, c_ctx[None, :], jnp.zeros((pad, d), c.dtype)], axis=0).astype(F32)
    mod = _modulation(cs, ada_w.astype(F32), ada_b.astype(F32))
    lb_all = _lower_bounds(hg_lb_logits)
    cos_t, sin_t = _rope_tables(n_lat, n_ctx)

    c_hg = q_rank + kv_rank + MLA_ROPE + width
    c_cv = c_hg + 2 * kw + vw + kw + vw
    c_gate = c_cv + 4 * cw
    w_in_b = w_in.astype(BF16)
    w_branch_b = w_branch.astype(BF16)
    w_out_b = w_out.astype(BF16)
    final_g = final_norm_g.reshape(1, d).astype(F32)

    h = jnp.concatenate([x, ctx], axis=1).astype(F32)
    tm_norm = 256
    tm_merge = 384 if n % 384 == 0 else ROW_TILE
    tm_last = 512 if n_lat % 512 == 0 else ROW_TILE
    for l in range(depth):
        last = l == depth - 1
        mod3 = mod[l].reshape(mod.shape[1], 1, 3 * d)
        u = _norm_mod(h, mod3, norm_g[l].astype(F32), n_lat, tm_norm)
        w_mla, wuq2, wukn, wvt = _mla_weights(w_in_b[l], mla_w_uq[l].astype(BF16), mla_w_ukv[l].astype(BF16),
                                              q_rank, kv_rank)
        y_mla = _mla(u, w_mla, mla_q_norm_g[l].reshape(1, q_rank).astype(F32),
                     mla_kv_norm_g[l].reshape(1, kv_rank).astype(F32), wuq2, wukn, wvt, cos_t, sin_t, n_lat)
        y_hg = _hgrn(u, w_in_b[l, :, c_hg:c_cv], lb_all[l].reshape(1, 2 * kw),
                     hg_norm_g[l].reshape(1, vw).astype(F32), n_lat)
        y_cv = _conv(u, w_in_b[l, :, c_cv:c_gate], conv_w[l].astype(F32), conv_b[l].reshape(1, cw).astype(F32), n_lat)
        h = _merge(u, y_mla, y_hg, y_cv, h, mod3, w_in_b[l, :, c_gate:], w_branch_b[l], w_out_b[l], final_g,
                   n_lat, tm_last if last else tm_merge, last)
    return h
```

```python
import functools

import numpy as np
import jax
import jax.numpy as jnp
from jax import lax
from jax.experimental import pallas as pl
from jax.experimental.pallas import tpu as pltpu

F32 = jnp.float32
BF16 = jnp.bfloat16

RMS_EPS = 1e-6
GRID_W = 64
ROPE_BASE = 10000.0
MLA_HEADS = 8
MLA_NOPE = 64
MLA_ROPE = 32
MLA_V = 64
MLA_SLOT = 128
MLA_SCALE = (MLA_NOPE + MLA_ROPE) ** -0.5
LOG2_E = float(np.log2(np.e))
MLA_SCALE_LOG2 = MLA_SCALE * LOG2_E
HG_HEADS = 8
HG_DK = 64
HG_DV = 64
HG_GROUP = 256
HG_CHUNK = 32
CV_K = 3

V7X_VMEM_BYTES = 64 * 1024 * 1024
V7X_LANES = 128
ROW_TILE = 256


def _vmem_limit(nbytes):
    return int(min(nbytes + (12 << 20), V7X_VMEM_BYTES - (6 << 20)))


def _dot(a, b):
    return jnp.dot(a, b, preferred_element_type=F32)


def _dot_nt(a, b):
    return lax.dot_general(a, b, (((1,), (1,)), ((), ())), preferred_element_type=F32)


def _sigmoid(x):
    return 1.0 / (1.0 + jnp.exp(-x))


def _silu(x):
    return x * _sigmoid(x)


def _split2(x):
    hi = x.astype(BF16)
    lo = (x - hi.astype(F32)).astype(BF16)
    return hi, lo


def _mod_norm(x, g, mod_b, mod_c, is_lat):
    d = x.shape[-1]
    y = x * lax.rsqrt(jnp.mean(x * x, axis=-1, keepdims=True) + RMS_EPS) * g
    shift = jnp.where(is_lat, mod_b[:, 0:d], mod_c[:, 0:d])
    scale = jnp.where(is_lat, mod_b[:, d:2 * d], mod_c[:, d:2 * d])
    return (y * (1.0 + scale) + shift).astype(BF16)


def _mod_kernel(c_ref, w_ref, b_ref, o_ref):
    s = _silu(c_ref[...])
    o_ref[0] = jnp.dot(s, w_ref[0], preferred_element_type=F32, precision=lax.Precision.HIGHEST) + b_ref[0]


def _modulation(cs, ada_w, ada_b):
    depth, d, d3 = ada_w.shape
    rows = cs.shape[0]
    nb = d3 // d
    return pl.pallas_call(
        _mod_kernel,
        out_shape=jax.ShapeDtypeStruct((depth, rows, d3), F32),
        grid=(depth, nb),
        in_specs=[
            pl.BlockSpec((rows, d), lambda l, j: (0, 0)),
            pl.BlockSpec((1, d, d), lambda l, j: (l, 0, j)),
            pl.BlockSpec((1, 1, d), lambda l, j: (l, 0, j)),
        ],
        out_specs=pl.BlockSpec((1, rows, d), lambda l, j: (l, 0, j)),
        compiler_params=pltpu.CompilerParams(
            dimension_semantics=("arbitrary", "arbitrary"),
            vmem_limit_bytes=_vmem_limit(2 * d * d * 4 + 4 * rows * d * 4),
        ),
        name="adaln_mod",
    )(cs, ada_w, ada_b.reshape(depth, 1, d3))


def _lb_kernel(x_ref, o_ref):
    x = x_ref[...]
    m = jnp.max(x, axis=0, keepdims=True)
    e = jnp.exp(x - m)
    p = e / jnp.sum(e, axis=0, keepdims=True)
    depth = x.shape[0]
    rows = [p[0:1]]
    for l in range(1, depth):
        rows.append(rows[-1] + p[l:l + 1])
    cs = jnp.concatenate(rows, axis=0)
    o_ref[...] = cs - cs[0:1]


def _lower_bounds(hg_lb_logits):
    depth = hg_lb_logits.shape[0]
    flat = hg_lb_logits.reshape(depth, -1).astype(F32)
    return pl.pallas_call(
        _lb_kernel,
        out_shape=jax.ShapeDtypeStruct(flat.shape, F32),
        name="hgrn_lower_bounds",
    )(flat)


def _norm_kernel(h_ref, modb_ref, modc_ref, g_ref, u_ref, *, n_lat, tm):
    row = pl.program_id(1) * tm + lax.broadcasted_iota(jnp.int32, (tm, 1), 0)
    u_ref[0] = _mod_norm(h_ref[0], g_ref[...], modb_ref[0], modc_ref[0], row < n_lat)


def _norm_mod(h, mod3, norm_g, n_lat, tm):
    bsz, n, d = h.shape
    ctx_row = bsz
    return pl.pallas_call(
        functools.partial(_norm_kernel, n_lat=n_lat, tm=tm),
        out_shape=jax.ShapeDtypeStruct((bsz, n, d), BF16),
        grid=(bsz, n // tm),
        in_specs=[
            pl.BlockSpec((1, tm, d), lambda b, j: (b, j, 0)),
            pl.BlockSpec((1, 1, 3 * d), lambda b, j: (b, 0, 0)),
            pl.BlockSpec((1, 1, 3 * d), lambda b, j: (ctx_row, 0, 0)),
            pl.BlockSpec((1, d), lambda b, j: (0, 0)),
        ],
        out_specs=pl.BlockSpec((1, tm, d), lambda b, j: (b, j, 0)),
        compiler_params=pltpu.CompilerParams(
            dimension_semantics=("arbitrary", "arbitrary"),
            vmem_limit_bytes=_vmem_limit(2 * tm * d * 6 + 4 * tm * d * 4),
        ),
        name="norm_mod",
    )(h, mod3, mod3, norm_g.reshape(1, d))


def _mla_kernel(u_ref, w_ref, gq_ref, gkv_ref, wuq_ref, wukn_ref, wvt_ref, cos_ref, sin_ref, out_ref,
                q_s, k_s, vt_s, g_s, ot_s, st_s, *, n_lat, n_out, q_rank, kv_rank):
    n = u_ref.shape[1]
    n_ctx = n - n_lat
    hs = MLA_HEADS * MLA_SLOT
    width = MLA_HEADS * MLA_V
    c_kv = q_rank
    c_kr = q_rank + kv_rank
    c_krs = c_kr + MLA_SLOT
    c_g = c_krs + MLA_SLOT

    for t in range(n // ROW_TILE):
        rows = slice(t * ROW_TILE, (t + 1) * ROW_TILE)
        z = _dot(u_ref[0, rows, :], w_ref[...])
        cos_t = cos_ref[rows, :]
        sin_t = sin_ref[rows, :]
        cos8 = jnp.concatenate([cos_t] * MLA_HEADS, axis=1)
        sin8 = jnp.concatenate([sin_t] * MLA_HEADS, axis=1)
        ckv = z[:, c_kv:c_kr]
        kvn = (ckv * lax.rsqrt(jnp.mean(ckv * ckv, axis=-1, keepdims=True) + RMS_EPS) * gkv_ref[...]).astype(BF16)
        kn = _dot(kvn, wukn_ref[...])
        kr = z[:, c_kr:c_krs] * cos_t + z[:, c_krs:c_g] * sin_t
        k = (kn + jnp.concatenate([kr] * MLA_HEADS, axis=1)).astype(BF16)
        for h in range(MLA_HEADS):
            k_s[h, rows, :] = k[:, h * MLA_SLOT:(h + 1) * MLA_SLOT]
        vt = _dot_nt(wvt_ref[...], kvn).astype(BF16)
        for h in range(MLA_HEADS):
            vt_s[h, :, rows] = vt[h * MLA_V:(h + 1) * MLA_V, :]
        if t * ROW_TILE < n_out:
            cq = z[:, 0:q_rank]
            cqn = (cq * lax.rsqrt(jnp.mean(cq * cq, axis=-1, keepdims=True) + RMS_EPS) * gq_ref[...]).astype(BF16)
            q2 = _dot(cqn, wuq_ref[...])
            q = ((q2[:, 0:hs] * cos8 + q2[:, hs:2 * hs] * sin8) * MLA_SCALE_LOG2).astype(BF16)
            for h in range(MLA_HEADS):
                q_s[h, rows, :] = q[:, h * MLA_SLOT:(h + 1) * MLA_SLOT]
            g_s[rows, :] = _silu(z[:, c_g:c_g + width])

    def attend(r0, k0, nk):
        rows = pl.ds(r0, ROW_TILE)

        def scores(h):
            st = _dot_nt(k_s[h, k0:k0 + nk, :], q_s[h, rows, :])
            st_s[h % 2, 0:nk, :] = st
            return jnp.max(st, axis=0, keepdims=True)

        m_next = scores(0)
        for h in range(MLA_HEADS):
            m = m_next
            if h + 1 < MLA_HEADS:
                m_next = scores(h + 1)
            p = jnp.exp2(st_s[h % 2, 0:nk, :] - m)
            l = jnp.sum(p, axis=0, keepdims=True)
            ot = _dot(vt_s[h, :, k0:k0 + nk], p.astype(BF16))
            ot_s[h] = ot * (1.0 / l)
        o = jnp.concatenate([ot_s[h] for h in range(MLA_HEADS)], axis=0).T
        out_ref[0, rows, :] = (o * g_s[rows, :]).astype(BF16)

    def lat_tile(t, carry):
        attend(pl.multiple_of(t * ROW_TILE, ROW_TILE), 0, n)
        return carry

    lax.fori_loop(0, n_lat // ROW_TILE, lat_tile, 0)
    for t in range((n_out - n_lat) // ROW_TILE):
        attend(n_lat + t * ROW_TILE, n_lat, n_ctx)


def _mla(u, w_mla, gq, gkv, wuq2, wukn, wvt, cos_t, sin_t, n_lat, n_out):
    bsz, n, d = u.shape
    q_rank, kv_rank = gq.shape[-1], gkv.shape[-1]
    width = MLA_HEADS * MLA_V
    hs = MLA_HEADS * MLA_SLOT
    est = (2 * n * d * 2 + 2 * w_mla.size * 2 + 2 * (wuq2.size + wukn.size + wvt.size) * 2
           + 4 * n * MLA_SLOT * 4 + 2 * n_out * width * 2
           + 2 * n * hs * 2 + n * width * 2 + n * width * 4 + width * ROW_TILE * 4
           + 2 * n * ROW_TILE * 4 + 2 * n * ROW_TILE * 4)
    full = lambda a: pl.BlockSpec(a.shape, lambda b: (0,) * a.ndim)
    return pl.pallas_call(
        functools.partial(_mla_kernel, n_lat=n_lat, n_out=n_out, q_rank=q_rank, kv_rank=kv_rank),
        out_shape=jax.ShapeDtypeStruct((bsz, n_out, width), BF16),
        grid=(bsz,),
        in_specs=[
            pl.BlockSpec((1, n, d), lambda b: (b, 0, 0)),
            full(w_mla), full(gq), full(gkv), full(wuq2), full(wukn), full(wvt), full(cos_t), full(sin_t),
        ],
        out_specs=pl.BlockSpec((1, n_out, width), lambda b: (b, 0, 0)),
        scratch_shapes=[
            pltpu.VMEM((MLA_HEADS, n, MLA_SLOT), BF16),
            pltpu.VMEM((MLA_HEADS, n, MLA_SLOT), BF16),
            pltpu.VMEM((MLA_HEADS, MLA_V, n), BF16),
            pltpu.VMEM((n, width), F32),
            pltpu.VMEM((MLA_HEADS, MLA_V, ROW_TILE), F32),
            pltpu.VMEM((2, n, ROW_TILE), F32),
        ],
        compiler_params=pltpu.CompilerParams(
            dimension_semantics=("arbitrary",), vmem_limit_bytes=_vmem_limit(est)),
        name="mla_mixer",
    )(u, w_mla, gq, gkv, wuq2, wukn, wvt, cos_t, sin_t)


def _hgrn_kernel(u_ref, w_ref, lb_ref, gn_ref, out_ref,
                 o_s, st_s, q_s, v_s, gate_s, k_s, b_s, dt_s, *, n_lat, n_out):
    n = u_ref.shape[1]
    kw = HG_HEADS * HG_DK
    vw = HG_HEADS * HG_DV
    c_f = kw + vw
    c_gate = c_f + 2 * kw
    n_tiles = n // ROW_TILE
    lat_tiles = n_lat // ROW_TILE
    ctx_tiles = n_tiles - lat_tiles
    n_chunk = ROW_TILE // HG_CHUNK
    n_grp = kw // HG_GROUP
    heads_per_grp = HG_GROUP // HG_DK
    stack = heads_per_grp * HG_CHUNK

    ti = lax.broadcasted_iota(jnp.int32, (ROW_TILE, ROW_TILE), 0)
    si = lax.broadcasted_iota(jnp.int32, (ROW_TILE, ROW_TILE), 1)
    same = (ti // HG_CHUNK) == (si // HG_CHUNK)
    ci = lax.broadcasted_iota(jnp.int32, (16, ROW_TILE), 0)
    cs_ = lax.broadcasted_iota(jnp.int32, (16, ROW_TILE), 1) // HG_CHUNK
    chunk_sum = jnp.where(ci == cs_, 1.0, 0.0)
    tri = tuple(jnp.concatenate([jnp.where(same & m, 1.0, 0.0), chunk_sum], axis=0).astype(BF16)
                for m in (si <= ti, si >= ti))
    hr = lax.broadcasted_iota(jnp.int32, (stack, HG_GROUP), 0) // HG_CHUNK
    hc = lax.broadcasted_iota(jnp.int32, (stack, HG_GROUP), 1) // HG_DK
    head_mask_f = jnp.where(hr == hc, 1.0, 0.0)
    head_mask = head_mask_f.astype(BF16)
    at = lax.broadcasted_iota(jnp.int32, (HG_CHUNK, stack), 0)
    as_ = lax.broadcasted_iota(jnp.int32, (HG_CHUNK, stack), 1) % HG_CHUNK
    causal = (as_ <= at, as_ >= at)
    hq = HG_GROUP // 2
    zq = jnp.zeros((hq, hq), BF16)
    ref_row = (HG_CHUNK // 2 - 1, HG_CHUNK // 2)
    end_row = (HG_CHUNK - 1, 0)

    lb_all = lb_ref[...]

    def qvg(t, carry):
        rows = pl.ds(pl.multiple_of(t * ROW_TILE, ROW_TILE), ROW_TILE)
        ut = u_ref[0, rows, :]
        qv = _dot(ut, w_ref[:, 0:c_f])
        q_s[rows, :] = qv[:, 0:kw]
        v_s[rows, :] = qv[:, kw:c_f].astype(BF16)
        gate_s[rows, :] = _silu(_dot(ut, w_ref[:, c_gate:c_gate + vw])).astype(BF16)
        return carry

    lax.fori_loop(0, n_tiles, qvg, 0)

    def tiles_of(step):
        t_fwd = jnp.where(step < ctx_tiles, lat_tiles + step, step - ctx_tiles)
        t_bwd = n_tiles - 1 - step
        return (t_fwd, t_bwd)

    def precompute(step, slot):
        tl = tiles_of(step)
        for d in range(2):
            idx = slot * 2 + d
            rows = pl.ds(pl.multiple_of(tl[d] * ROW_TILE, ROW_TILE), ROW_TILE)
            zf = _dot(u_ref[0, rows, :], w_ref[:, c_f + d * kw:c_f + (d + 1) * kw])
            lb = lb_all[:, d * kw:(d + 1) * kw]
            e = jnp.exp(-jnp.abs(zf))
            log_sig = jnp.minimum(zf, 0.0) - jnp.log(1.0 + e)
            a = jnp.log(lb)
            c = jnp.log(1.0 - lb) + log_sig
            log_f = jnp.maximum(a, c) + jnp.log(1.0 + jnp.exp(-jnp.abs(a - c)))
            r = 1.0 / (1.0 + e)
            k_s[idx] = (1.0 - lb) * jnp.where(zf >= 0.0, e * r, r)
            g_hi, g_lo = _split2(log_f)
            cum = (_dot(tri[d], g_hi) + _dot(tri[d], g_lo)) * LOG2_E
            b_s[idx] = cum[0:ROW_TILE]
            tot = cum[ROW_TILE:ROW_TILE + n_chunk]
            dt = jnp.concatenate([jnp.exp2(tot), jnp.zeros((V7X_LANES - n_chunk, kw), F32)], axis=0)
            dt_s[idx] = dt.T

    def chunks(step, slot):
        tl = tiles_of(step)
        for cidx in range(n_chunk):
            for d in range(2):
                idx = slot * 2 + d
                cl = cidx if d == 0 else n_chunk - 1 - cidx
                c0 = cl * HG_CHUNK
                rows = slice(c0, c0 + HG_CHUNK)
                grow = pl.ds(pl.multiple_of(tl[d] * ROW_TILE + c0, HG_CHUNK), HG_CHUNK)
                for g in range(n_grp):
                    lanes = slice(g * HG_GROUP, (g + 1) * HG_GROUP)
                    q = q_s[grow, lanes]
                    v = v_s[grow, lanes]
                    k = k_s[idx, rows, lanes]
                    b = b_s[idx, rows, lanes]
                    b_ref = b_s[idx, c0 + ref_row[d]:c0 + ref_row[d] + 1, lanes]
                    b_end = b_s[idx, c0 + end_row[d]:c0 + end_row[d] + 1, lanes]
                    qd = (q * jnp.exp2(b - b_ref)).astype(BF16)
                    qs = (q * jnp.exp2(b)).astype(BF16)
                    kd = (k * jnp.exp2(b_ref - b)).astype(BF16)
                    kr = k * jnp.exp2(b_end - b)
                    kd_st = jnp.concatenate([kd] * heads_per_grp, axis=0) * head_mask
                    v_st = jnp.concatenate([v] * heads_per_grp, axis=0) * head_mask
                    kr_t = (jnp.concatenate([kr] * heads_per_grp, axis=0) * head_mask_f).T.astype(BF16)
                    amat = jnp.where(causal[d], _dot_nt(qd, kd_st), 0.0).astype(BF16)
                    res = _dot(jnp.concatenate([kr_t, amat], axis=0), v_st)
                    sidx = d * n_grp + g
                    dtc = dt_s[idx, lanes, cl:cl + 1]
                    s0 = st_s[sidx, 0]
                    s1 = st_s[sidx, 1]
                    s_bf = jnp.concatenate(
                        [jnp.concatenate([s0.astype(BF16), zq], axis=1),
                         jnp.concatenate([zq, s1.astype(BF16)], axis=1)], axis=0)
                    o_s[d, grow, lanes] = res[HG_GROUP:] + _dot(qs, s_bf)
                    st_s[sidx, 0] = s0 * dtc[0:hq] + res[0:hq, 0:hq]
                    st_s[sidx, 1] = s1 * dtc[hq:HG_GROUP] + res[hq:HG_GROUP, hq:HG_GROUP]

    st_s[...] = jnp.zeros_like(st_s)
    precompute(0, 0)

    def step_pair(j, carry):
        i = 2 * j
        precompute(i + 1, 1)
        chunks(i, 0)
        precompute(i + 2, 0)
        chunks(i + 1, 1)
        return carry

    n_pairs = (n_tiles - 1) // 2
    lax.fori_loop(0, n_pairs, step_pair, 0)
    if 2 * n_pairs == n_tiles - 1:
        chunks(n_tiles - 1, 0)
    else:
        precompute(n_tiles - 1, 1)
        chunks(n_tiles - 2, 0)
        chunks(n_tiles - 1, 1)

    hi_ = lax.broadcasted_iota(jnp.int32, (vw, vw), 0) // HG_DV
    hj_ = lax.broadcasted_iota(jnp.int32, (vw, vw), 1) // HG_DV
    head_sum = jnp.where(hi_ == hj_, 1.0, 0.0).astype(BF16)

    def finish(t, carry):
        rows = pl.ds(pl.multiple_of(t * ROW_TILE, ROW_TILE), ROW_TILE)
        o = o_s[0, rows, :] + o_s[1, rows, :]
        sq_hi, sq_lo = _split2(o * o)
        ms = (_dot(sq_hi, head_sum) + _dot(sq_lo, head_sum)) * (1.0 / HG_DV)
        y = o * lax.rsqrt(ms + RMS_EPS) * gn_ref[...]
        out_ref[0, rows, :] = (y * gate_s[rows, :].astype(F32)).astype(BF16)
        return carry

    lax.fori_loop(0, n_out // ROW_TILE, finish, 0)


def _hgrn(u, w_hg, lb, gn, n_lat, n_out):
    bsz, n, d = u.shape
    kw = HG_HEADS * HG_DK
    vw = HG_HEADS * HG_DV
    n_streams = 2 * (kw // HG_GROUP)
    est = (2 * n * d * 2 + 2 * w_hg.size * 2 + 2 * n_out * vw * 2
           + 2 * n * vw * 4 + n_streams * HG_GROUP * HG_GROUP * 2 + n * kw * 4 + n * 2 * vw * 2
           + 8 * ROW_TILE * kw * 4 + 4 * kw * V7X_LANES * 4 + 6 * ROW_TILE * kw * 4)
    full = lambda a: pl.BlockSpec(a.shape, lambda b: (0,) * a.ndim)
    return pl.pallas_call(
        functools.partial(_hgrn_kernel, n_lat=n_lat, n_out=n_out),
        out_shape=jax.ShapeDtypeStruct((bsz, n_out, vw), BF16),
        grid=(bsz,),
        in_specs=[pl.BlockSpec((1, n, d), lambda b: (b, 0, 0)), full(w_hg), full(lb), full(gn)],
        out_specs=pl.BlockSpec((1, n_out, vw), lambda b: (b, 0, 0)),
        scratch_shapes=[
            pltpu.VMEM((2, n, vw), F32),
            pltpu.VMEM((n_streams, 2, HG_GROUP // 2, HG_GROUP // 2), F32),
            pltpu.VMEM((n, kw), F32),
            pltpu.VMEM((n, vw), BF16),
            pltpu.VMEM((n, vw), BF16),
            pltpu.VMEM((4, ROW_TILE, kw), F32),
            pltpu.VMEM((4, ROW_TILE, kw), F32),
            pltpu.VMEM((4, kw, V7X_LANES), F32),
        ],
        compiler_params=pltpu.CompilerParams(
            dimension_semantics=("arbitrary",), vmem_limit_bytes=_vmem_limit(est)),
        name="hgrn_mixer",
    )(u, w_hg, lb, gn)


CV_PAD = 8


def _conv_kernel(u_ref, w_ref, cw_ref, cb_ref, out_ref, uu_s, coef_s, *, n_lat, n_out):
    n = u_ref.shape[1]
    cw = cw_ref.shape[-1]
    out_tiles = n_out // ROW_TILE
    lat_tiles = n_lat // ROW_TILE
    zero = jnp.zeros((CV_PAD, cw), F32)
    uu_s[0:CV_PAD, :] = zero
    uu_s[CV_PAD + n_lat:2 * CV_PAD + n_lat, :] = zero
    uu_s[2 * CV_PAD + n:3 * CV_PAD + n, :] = zero

    def off(t):
        return CV_PAD + t * ROW_TILE + (CV_PAD if t >= lat_tiles else 0)

    for t in range(out_tiles):
        rows = slice(t * ROW_TILE, (t + 1) * ROW_TILE)
        z = _dot(u_ref[0, rows, :], w_ref[...])
        uu_s[off(t):off(t) + ROW_TILE, :] = z[:, 2 * cw:3 * cw] * z[:, 0:cw]
        coef_s[rows, :] = z[:, cw:2 * cw] * _silu(z[:, 3 * cw:4 * cw])
    w0 = cw_ref[0:1, :]
    w1 = cw_ref[1:2, :]
    w2 = cw_ref[2:3, :]
    bias = cb_ref[...]
    for t in range(out_tiles):
        rows = slice(t * ROW_TILE, (t + 1) * ROW_TILE)
        o = off(t)
        conv = (uu_s[o - 1:o - 1 + ROW_TILE, :] * w0 + uu_s[o:o + ROW_TILE, :] * w1
                + uu_s[o + 1:o + 1 + ROW_TILE, :] * w2 + bias)
        out_ref[0, rows, :] = (coef_s[rows, :] * conv).astype(BF16)


def _conv(u, w_cv, conv_w, conv_b, n_lat, n_out):
    bsz, n, d = u.shape
    cw = conv_w.shape[-1]
    est = (2 * n * d * 2 + 2 * w_cv.size * 2 + 2 * n_out * cw * 2 + (n + 3 * CV_PAD) * cw * 4 + n * cw * 4
           + 4 * ROW_TILE * 4 * cw * 4)
    full = lambda a: pl.BlockSpec(a.shape, lambda b: (0,) * a.ndim)
    return pl.pallas_call(
        functools.partial(_conv_kernel, n_lat=n_lat, n_out=n_out),
        out_shape=jax.ShapeDtypeStruct((bsz, n_out, cw), BF16),
        grid=(bsz,),
        in_specs=[pl.BlockSpec((1, n, d), lambda b: (b, 0, 0)), full(w_cv), full(conv_w), full(conv_b)],
        out_specs=pl.BlockSpec((1, n_out, cw), lambda b: (b, 0, 0)),
        scratch_shapes=[pltpu.VMEM((n + 3 * CV_PAD, cw), F32), pltpu.VMEM((n, cw), F32)],
        compiler_params=pltpu.CompilerParams(
            dimension_semantics=("arbitrary",), vmem_limit_bytes=_vmem_limit(est)),
        name="conv_mixer",
    )(u, w_cv, conv_w, conv_b)


def _merge_kernel(u_ref, ym_ref, yh_ref, yc_ref, h_ref, modb_ref, modc_ref, wg_ref, wb_ref, wo_ref,
                  nmodb_ref, nmodc_ref, ng_ref, *out_refs, n_lat, tm, last):
    d = h_ref.shape[-1]
    u = u_ref[0]
    acc = None
    for i, y_ref in enumerate((ym_ref, yh_ref, yc_ref)):
        gate_i = _sigmoid(_dot(u, wg_ref[:, i * d:(i + 1) * d]))
        term = gate_i * _dot(y_ref[0], wb_ref[i])
        acc = term if acc is None else acc + term
    out = _dot(acc.astype(BF16), wo_ref[...])
    row = pl.program_id(1) * tm + lax.broadcasted_iota(jnp.int32, (tm, 1), 0)
    is_lat = row < n_lat
    gate = jnp.where(is_lat, modb_ref[0][:, 2 * d:3 * d], modc_ref[0][:, 2 * d:3 * d])
    hn = h_ref[0] + gate * out
    if last:
        o_ref, = out_refs
        o_ref[0] = hn * lax.rsqrt(jnp.mean(hn * hn, axis=-1, keepdims=True) + RMS_EPS) * ng_ref[...]
    else:
        o_ref, un_ref = out_refs
        o_ref[0] = hn
        un_ref[0] = _mod_norm(hn, ng_ref[...], nmodb_ref[0], nmodc_ref[0], is_lat)


def _merge(u, ym, yh, yc, h, mod3, w_gate, w_branch, w_out, next_mod3, next_g, n_lat, tm, last):
    bsz, n, d = h.shape
    bw = ym.shape[-1]
    n_out = n_lat if last else n
    ctx_row = bsz
    tok = lambda w: pl.BlockSpec((1, tm, w), lambda b, j: (b, j, 0))
    modb = pl.BlockSpec((1, 1, 3 * d), lambda b, j: (b, 0, 0))
    modc = pl.BlockSpec((1, 1, 3 * d), lambda b, j: (ctx_row, 0, 0))
    full = lambda a: pl.BlockSpec(a.shape, lambda b, j: (0,) * a.ndim)
    est = (2 * tm * (d * 2 + 3 * bw * 2 + 2 * d * 4 + d * 2) + 2 * (w_gate.size + w_branch.size + w_out.size) * 2
           + 6 * tm * d * 4)
    out_shape = [jax.ShapeDtypeStruct((bsz, n_out, d), F32)]
    out_specs = [tok(d)]
    if not last:
        out_shape.append(jax.ShapeDtypeStruct((bsz, n_out, d), BF16))
        out_specs.append(tok(d))
    return pl.pallas_call(
        functools.partial(_merge_kernel, n_lat=n_lat, tm=tm, last=last),
        out_shape=out_shape,
        grid=(bsz, n_out // tm),
        in_specs=[
            tok(d), tok(bw), tok(bw), tok(bw), tok(d), modb, modc,
            full(w_gate), full(w_branch), full(w_out), modb, modc, full(next_g),
        ],
        out_specs=out_specs,
        compiler_params=pltpu.CompilerParams(
            dimension_semantics=("arbitrary", "arbitrary"), vmem_limit_bytes=_vmem_limit(est)),
        name="merge_out",
    )(u, ym, yh, yc, h, mod3, mod3, w_gate, w_branch, w_out, next_mod3, next_mod3, next_g)


def _rope_tables(n_lat, n_ctx):
    pairs = MLA_ROPE // 4
    rows = n_lat // GRID_W
    row_id = np.repeat(np.arange(rows, dtype=np.float32), GRID_W)
    col_id = np.tile(np.arange(GRID_W, dtype=np.float32), rows)
    inv_freq = jnp.power(ROPE_BASE, -jnp.arange(pairs, dtype=F32) / pairs)
    ang = jnp.stack([row_id[:, None] * inv_freq, col_id[:, None] * inv_freq], axis=1)
    ang = jnp.broadcast_to(ang[:, :, None, :], (n_lat, 2, 2, pairs)).reshape(n_lat, MLA_ROPE)
    pad_hi = MLA_SLOT - MLA_NOPE - MLA_ROPE
    cos = jnp.concatenate([jnp.ones((n_lat, MLA_NOPE), F32), jnp.cos(ang), jnp.ones((n_lat, pad_hi), F32)], axis=1)
    sin = jnp.concatenate([jnp.zeros((n_lat, MLA_NOPE), F32), jnp.sin(ang), jnp.zeros((n_lat, pad_hi), F32)], axis=1)
    cos = jnp.concatenate([cos, jnp.ones((n_ctx, MLA_SLOT), F32)], axis=0)
    sin = jnp.concatenate([sin, jnp.zeros((n_ctx, MLA_SLOT), F32)], axis=0)
    return cos, sin


def _rotate_half_cols(w):
    pairs = MLA_ROPE // 4
    ws = w.reshape(w.shape[:-1] + (2, 2, pairs))
    return jnp.stack([-ws[..., 1, :], ws[..., 0, :]], axis=-2).reshape(w.shape)


def _mla_weights(w_in_l, w_uq_l, w_ukv_l, q_rank, kv_rank):
    d = w_in_l.shape[0]
    width = MLA_HEADS * MLA_V
    pad_hi = MLA_SLOT - MLA_NOPE - MLA_ROPE
    c_kr = q_rank + kv_rank
    w_kr = w_in_l[:, c_kr:c_kr + MLA_ROPE]
    slot = lambda w: jnp.concatenate([jnp.zeros((d, MLA_NOPE), w.dtype), w, jnp.zeros((d, pad_hi), w.dtype)], axis=1)
    w_mla = jnp.concatenate([w_in_l[:, 0:c_kr], slot(w_kr), slot(_rotate_half_cols(w_kr)),
                             w_in_l[:, c_kr + MLA_ROPE:c_kr + MLA_ROPE + width]], axis=1)
    uq = w_uq_l.reshape(q_rank, MLA_HEADS, MLA_NOPE + MLA_ROPE)
    uq_n, uq_r = uq[..., :MLA_NOPE], uq[..., MLA_NOPE:]
    zn = jnp.zeros_like(uq_n)
    zp = jnp.zeros((q_rank, MLA_HEADS, pad_hi), uq.dtype)
    main = jnp.concatenate([uq_n, uq_r, zp], axis=-1).reshape(q_rank, MLA_HEADS * MLA_SLOT)
    part = jnp.concatenate([zn, _rotate_half_cols(uq_r), zp], axis=-1).reshape(q_rank, MLA_HEADS * MLA_SLOT)
    wuq2 = jnp.concatenate([main, part], axis=1)
    ukv = w_ukv_l.reshape(kv_rank, MLA_HEADS, MLA_NOPE + MLA_V)
    kn = ukv[..., :MLA_NOPE]
    wukn = jnp.concatenate([kn, jnp.zeros((kv_rank, MLA_HEADS, MLA_SLOT - MLA_NOPE), kn.dtype)], axis=-1)
    wukn = wukn.reshape(kv_rank, MLA_HEADS * MLA_SLOT)
    wvt = ukv[..., MLA_NOPE:].reshape(kv_rank, width).T
    return w_mla, wuq2, wukn, wvt


def kernel(x, c, ctx, c_ctx, ada_w, ada_b, norm_g, w_in, mla_q_norm_g, mla_kv_norm_g, mla_w_uq, mla_w_ukv,
           hg_lb_logits, hg_norm_g, conv_w, conv_b, w_branch, w_out, final_norm_g):
    bsz, n_lat, d = x.shape
    n_ctx = ctx.shape[1]
    n = n_lat + n_ctx
    depth = w_in.shape[0]
    q_rank = mla_q_norm_g.shape[-1]
    kv_rank = mla_kv_norm_g.shape[-1]
    width = MLA_HEADS * MLA_V
    kw = HG_HEADS * HG_DK
    vw = HG_HEADS * HG_DV
    cw = conv_w.shape[-1]
    assert n_lat % ROW_TILE == 0 and n_ctx % ROW_TILE == 0 and n_lat % GRID_W == 0
    assert width == vw == cw == w_branch.shape[2]

    pad = (-(bsz + 1)) % 8
    cs = jnp.concatenate([c, c_ctx[None, :], jnp.zeros((pad, d), c.dtype)], axis=0).astype(F32)
    mod = _modulation(cs, ada_w.astype(F32), ada_b.astype(F32))
    mod3 = [mod[l].reshape(mod.shape[1], 1, 3 * d) for l in range(depth)]
    lb_all = _lower_bounds(hg_lb_logits)
    cos_t, sin_t = _rope_tables(n_lat, n_ctx)

    c_hg = q_rank + kv_rank + MLA_ROPE + width
    c_cv = c_hg + 2 * kw + vw + kw + vw
    c_gate = c_cv + 4 * cw
    w_in_b = w_in.astype(BF16)
    w_branch_b = w_branch.astype(BF16)
    w_out_b = w_out.astype(BF16)
    norm_gs = [norm_g[l].reshape(1, d).astype(F32) for l in range(depth)]
    final_g = final_norm_g.reshape(1, d).astype(F32)

    h = jnp.concatenate([x, ctx], axis=1).astype(F32)
    tm_merge = 384 if n % 384 == 0 else ROW_TILE
    tm_last = 512 if n_lat % 512 == 0 else ROW_TILE
    u = _norm_mod(h, mod3[0], norm_gs[0], n_lat, ROW_TILE)
    for l in range(depth):
        last = l == depth - 1
        n_out = n_lat if last else n
        w_mla, wuq2, wukn, wvt = _mla_weights(w_in_b[l], mla_w_uq[l].astype(BF16), mla_w_ukv[l].astype(BF16),
                                              q_rank, kv_rank)
        y_mla = _mla(u, w_mla, mla_q_norm_g[l].reshape(1, q_rank).astype(F32),
                     mla_kv_norm_g[l].reshape(1, kv_rank).astype(F32), wuq2, wukn, wvt, cos_t, sin_t, n_lat, n_out)
        y_hg = _hgrn(u, w_in_b[l, :, c_hg:c_cv], lb_all[l].reshape(1, 2 * kw),
                     hg_norm_g[l].reshape(1, vw).astype(F32), n_lat, n_out)
        y_cv = _conv(u, w_in_b[l, :, c_cv:c_gate], conv_w[l].astype(F32), conv_b[l].reshape(1, cw).astype(F32),
                     n_lat, n_out)
        if last:
            h, = _merge(u, y_mla, y_hg, y_cv, h, mod3[l], w_in_b[l, :, c_gate:], w_branch_b[l], w_out_b[l],
                        mod3[l], final_g, n_lat, tm_last, True)
        else:
            h, u = _merge(u, y_mla, y_hg, y_cv, h, mod3[l], w_in_b[l, :, c_gate:], w_branch_b[l], w_out_b[l],
                          mod3[l + 1], norm_gs[l + 1], n_lat, tm_merge, False)
    return h
```

```python
import functools

import numpy as np
import jax
import jax.numpy as jnp
from jax import lax
from jax.experimental import pallas as pl
from jax.experimental.pallas import tpu as pltpu

F32 = jnp.float32
BF16 = jnp.bfloat16

RMS_EPS = 1e-6
GRID_W = 64
ROPE_BASE = 10000.0
MLA_HEADS = 8
MLA_NOPE = 64
MLA_ROPE = 32
MLA_V = 64
MLA_SLOT = 128
MLA_SCALE = (MLA_NOPE + MLA_ROPE) ** -0.5
LOG2_E = float(np.log2(np.e))
MLA_SCALE_LOG2 = MLA_SCALE * LOG2_E
HG_HEADS = 8
HG_DK = 64
HG_DV = 64
HG_GROUP = 256
HG_CHUNK = 32
CV_K = 3

V7X_VMEM_BYTES = 64 * 1024 * 1024
V7X_LANES = 128
ROW_TILE = 256


def _vmem_limit(nbytes):
    return int(min(nbytes + (12 << 20), V7X_VMEM_BYTES - (6 << 20)))


def _dot(a, b):
    return jnp.dot(a, b, preferred_element_type=F32)


def _dot_nt(a, b):
    return lax.dot_general(a, b, (((1,), (1,)), ((), ())), preferred_element_type=F32)


def _sigmoid(x):
    return 1.0 / (1.0 + jnp.exp(-x))


def _silu(x):
    return x * _sigmoid(x)


def _split2(x):
    hi = x.astype(BF16)
    lo = (x - hi.astype(F32)).astype(BF16)
    return hi, lo


def _mod_norm(x, g, mod_b, mod_c, is_lat):
    d = x.shape[-1]
    y = x * lax.rsqrt(jnp.mean(x * x, axis=-1, keepdims=True) + RMS_EPS) * g
    shift = jnp.where(is_lat, mod_b[:, 0:d], mod_c[:, 0:d])
    scale = jnp.where(is_lat, mod_b[:, d:2 * d], mod_c[:, d:2 * d])
    return (y * (1.0 + scale) + shift).astype(BF16)


def _mod_kernel(c_ref, w_ref, b_ref, o_ref):
    s = _silu(c_ref[...])
    o_ref[0] = jnp.dot(s, w_ref[0], preferred_element_type=F32, precision=lax.Precision.HIGHEST) + b_ref[0]


def _modulation(cs, ada_w, ada_b):
    depth, d, d3 = ada_w.shape
    rows = cs.shape[0]
    nb = d3 // d
    return pl.pallas_call(
        _mod_kernel,
        out_shape=jax.ShapeDtypeStruct((depth, rows, d3), F32),
        grid=(depth, nb),
        in_specs=[
            pl.BlockSpec((rows, d), lambda l, j: (0, 0)),
            pl.BlockSpec((1, d, d), lambda l, j: (l, 0, j)),
            pl.BlockSpec((1, 1, d), lambda l, j: (l, 0, j)),
        ],
        out_specs=pl.BlockSpec((1, rows, d), lambda l, j: (l, 0, j)),
        compiler_params=pltpu.CompilerParams(
            dimension_semantics=("arbitrary", "arbitrary"),
            vmem_limit_bytes=_vmem_limit(2 * d * d * 4 + 4 * rows * d * 4),
        ),
        name="adaln_mod",
    )(cs, ada_w, ada_b.reshape(depth, 1, d3))


def _lb_kernel(x_ref, o_ref):
    x = x_ref[...]
    m = jnp.max(x, axis=0, keepdims=True)
    e = jnp.exp(x - m)
    p = e / jnp.sum(e, axis=0, keepdims=True)
    depth = x.shape[0]
    rows = [p[0:1]]
    for l in range(1, depth):
        rows.append(rows[-1] + p[l:l + 1])
    cs = jnp.concatenate(rows, axis=0)
    o_ref[...] = cs - cs[0:1]


def _lower_bounds(hg_lb_logits):
    depth = hg_lb_logits.shape[0]
    flat = hg_lb_logits.reshape(depth, -1).astype(F32)
    return pl.pallas_call(
        _lb_kernel,
        out_shape=jax.ShapeDtypeStruct(flat.shape, F32),
        name="hgrn_lower_bounds",
    )(flat)


def _norm_kernel(h_ref, modb_ref, modc_ref, g_ref, u_ref, *, n_lat, tm):
    row = pl.program_id(1) * tm + lax.broadcasted_iota(jnp.int32, (tm, 1), 0)
    u_ref[0] = _mod_norm(h_ref[0], g_ref[...], modb_ref[0], modc_ref[0], row < n_lat)


def _norm_mod(h, mod3, norm_g, n_lat, tm):
    bsz, n, d = h.shape
    ctx_row = bsz
    return pl.pallas_call(
        functools.partial(_norm_kernel, n_lat=n_lat, tm=tm),
        out_shape=jax.ShapeDtypeStruct((bsz, n, d), BF16),
        grid=(bsz, n // tm),
        in_specs=[
            pl.BlockSpec((1, tm, d), lambda b, j: (b, j, 0)),
            pl.BlockSpec((1, 1, 3 * d), lambda b, j: (b, 0, 0)),
            pl.BlockSpec((1, 1, 3 * d), lambda b, j: (ctx_row, 0, 0)),
            pl.BlockSpec((1, d), lambda b, j: (0, 0)),
        ],
        out_specs=pl.BlockSpec((1, tm, d), lambda b, j: (b, j, 0)),
        compiler_params=pltpu.CompilerParams(
            dimension_semantics=("arbitrary", "arbitrary"),
            vmem_limit_bytes=_vmem_limit(2 * tm * d * 6 + 4 * tm * d * 4),
        ),
        name="norm_mod",
    )(h, mod3, mod3, norm_g.reshape(1, d))


def _mla_kernel(u_ref, w_ref, gq_ref, gkv_ref, wuq_ref, wukn_ref, wvt_ref, cos_ref, sin_ref, out_ref,
                q_s, k_s, vt_s, g_s, ot_s, st_s, *, n_lat, n_out, q_rank, kv_rank):
    n = u_ref.shape[1]
    n_ctx = n - n_lat
    hs = MLA_HEADS * MLA_SLOT
    width = MLA_HEADS * MLA_V
    c_kv = q_rank
    c_kr = q_rank + kv_rank
    c_krs = c_kr + MLA_SLOT
    c_g = c_krs + MLA_SLOT

    for t in range(n // ROW_TILE):
        rows = slice(t * ROW_TILE, (t + 1) * ROW_TILE)
        z = _dot(u_ref[0, rows, :], w_ref[...])
        cos_t = cos_ref[rows, :]
        sin_t = sin_ref[rows, :]
        cos8 = jnp.concatenate([cos_t] * MLA_HEADS, axis=1)
        sin8 = jnp.concatenate([sin_t] * MLA_HEADS, axis=1)
        ckv = z[:, c_kv:c_kr]
        kvn = (ckv * lax.rsqrt(jnp.mean(ckv * ckv, axis=-1, keepdims=True) + RMS_EPS) * gkv_ref[...]).astype(BF16)
        kn = _dot(kvn, wukn_ref[...])
        kr = z[:, c_kr:c_krs] * cos_t + z[:, c_krs:c_g] * sin_t
        k = (kn + jnp.concatenate([kr] * MLA_HEADS, axis=1)).astype(BF16)
        for h in range(MLA_HEADS):
            k_s[h, rows, :] = k[:, h * MLA_SLOT:(h + 1) * MLA_SLOT]
        vt = _dot_nt(wvt_ref[...], kvn).astype(BF16)
        for h in range(MLA_HEADS):
            vt_s[h, :, rows] = vt[h * MLA_V:(h + 1) * MLA_V, :]
        if t * ROW_TILE < n_out:
            cq = z[:, 0:q_rank]
            cqn = (cq * lax.rsqrt(jnp.mean(cq * cq, axis=-1, keepdims=True) + RMS_EPS) * gq_ref[...]).astype(BF16)
            q2 = _dot(cqn, wuq_ref[...])
            q = ((q2[:, 0:hs] * cos8 + q2[:, hs:2 * hs] * sin8) * MLA_SCALE_LOG2).astype(BF16)
            for h in range(MLA_HEADS):
                q_s[h, rows, :] = q[:, h * MLA_SLOT:(h + 1) * MLA_SLOT]
            g_s[rows, :] = _silu(z[:, c_g:c_g + width])

    def scores(h, r0, k0, nk):
        st = _dot_nt(k_s[h, k0:k0 + nk, :], q_s[h, pl.ds(r0, ROW_TILE), :])
        st_s[h % 2, 0:nk, :] = st
        return jnp.max(st, axis=0, keepdims=True)

    def attend(r0, k0, nk, m0, r0_next):
        rows = pl.ds(r0, ROW_TILE)
        m_next = m0
        for h in range(MLA_HEADS):
            m = m_next
            if h + 1 < MLA_HEADS:
                m_next = scores(h + 1, r0, k0, nk)
            elif r0_next is not None:
                m_next = scores(0, r0_next, k0, nk)
            p = jnp.exp2(st_s[h % 2, 0:nk, :] - m)
            l = jnp.sum(p, axis=0, keepdims=True)
            ot = _dot(vt_s[h, :, k0:k0 + nk], p.astype(BF16))
            ot_s[h] = ot * (1.0 / l)
        o = jnp.concatenate([ot_s[h] for h in range(MLA_HEADS)], axis=0).T
        out_ref[0, rows, :] = (o * g_s[rows, :]).astype(BF16)
        return m_next

    lat_tiles = n_lat // ROW_TILE

    def lat_tile(t, m0):
        r0 = pl.multiple_of(t * ROW_TILE, ROW_TILE)
        r0_next = pl.multiple_of(jnp.minimum(t + 1, lat_tiles - 1) * ROW_TILE, ROW_TILE)
        return attend(r0, 0, n, m0, r0_next)

    lax.fori_loop(0, lat_tiles, lat_tile, scores(0, 0, 0, n))
    for t in range((n_out - n_lat) // ROW_TILE):
        r0 = n_lat + t * ROW_TILE
        attend(r0, n_lat, n_ctx, scores(0, r0, n_lat, n_ctx), None)


def _mla(u, w_mla, gq, gkv, wuq2, wukn, wvt, cos_t, sin_t, n_lat, n_out):
    bsz, n, d = u.shape
    q_rank, kv_rank = gq.shape[-1], gkv.shape[-1]
    width = MLA_HEADS * MLA_V
    hs = MLA_HEADS * MLA_SLOT
    est = (2 * n * d * 2 + 2 * w_mla.size * 2 + 2 * (wuq2.size + wukn.size + wvt.size) * 2
           + 4 * n * MLA_SLOT * 4 + 2 * n_out * width * 2
           + 2 * n * hs * 2 + n * width * 2 + n * width * 4 + width * ROW_TILE * 4
           + 2 * n * ROW_TILE * 4 + 2 * n * ROW_TILE * 4)
    full = lambda a: pl.BlockSpec(a.shape, lambda b: (0,) * a.ndim)
    return pl.pallas_call(
        functools.partial(_mla_kernel, n_lat=n_lat, n_out=n_out, q_rank=q_rank, kv_rank=kv_rank),
        out_shape=jax.ShapeDtypeStruct((bsz, n_out, width), BF16),
        grid=(bsz,),
        in_specs=[
            pl.BlockSpec((1, n, d), lambda b: (b, 0, 0)),
            full(w_mla), full(gq), full(gkv), full(wuq2), full(wukn), full(wvt), full(cos_t), full(sin_t),
        ],
        out_specs=pl.BlockSpec((1, n_out, width), lambda b: (b, 0, 0)),
        scratch_shapes=[
            pltpu.VMEM((MLA_HEADS, n, MLA_SLOT), BF16),
            pltpu.VMEM((MLA_HEADS, n, MLA_SLOT), BF16),
            pltpu.VMEM((MLA_HEADS, MLA_V, n), BF16),
            pltpu.VMEM((n, width), F32),
            pltpu.VMEM((MLA_HEADS, MLA_V, ROW_TILE), F32),
            pltpu.VMEM((2, n, ROW_TILE), F32),
        ],
        compiler_params=pltpu.CompilerParams(
            dimension_semantics=("arbitrary",), vmem_limit_bytes=_vmem_limit(est)),
        name="mla_mixer",
    )(u, w_mla, gq, gkv, wuq2, wukn, wvt, cos_t, sin_t)


def _hgrn_kernel(u_ref, w_ref, lb_ref, gn_ref, out_ref,
                 o_s, st_s, q_s, v_s, gate_s, k_s, b_s, dt_s, *, n_lat, n_out):
    n = u_ref.shape[1]
    kw = HG_HEADS * HG_DK
    vw = HG_HEADS * HG_DV
    c_f = kw + vw
    c_gate = c_f + 2 * kw
    n_tiles = n // ROW_TILE
    lat_tiles = n_lat // ROW_TILE
    ctx_tiles = n_tiles - lat_tiles
    n_chunk = ROW_TILE // HG_CHUNK
    n_grp = kw // HG_GROUP
    heads_per_grp = HG_GROUP // HG_DK
    stack = heads_per_grp * HG_CHUNK

    ti = lax.broadcasted_iota(jnp.int32, (ROW_TILE, ROW_TILE), 0)
    si = lax.broadcasted_iota(jnp.int32, (ROW_TILE, ROW_TILE), 1)
    same = (ti // HG_CHUNK) == (si // HG_CHUNK)
    ci = lax.broadcasted_iota(jnp.int32, (16, ROW_TILE), 0)
    cs_ = lax.broadcasted_iota(jnp.int32, (16, ROW_TILE), 1) // HG_CHUNK
    chunk_sum = jnp.where(ci == cs_, 1.0, 0.0)
    tri = tuple(jnp.concatenate([jnp.where(same & m, 1.0, 0.0), chunk_sum], axis=0).astype(BF16)
                for m in (si <= ti, si >= ti))
    hr = lax.broadcasted_iota(jnp.int32, (stack, HG_GROUP), 0) // HG_CHUNK
    hc = lax.broadcasted_iota(jnp.int32, (stack, HG_GROUP), 1) // HG_DK
    head_mask_f = jnp.where(hr == hc, 1.0, 0.0)
    head_mask = head_mask_f.astype(BF16)
    at = lax.broadcasted_iota(jnp.int32, (HG_CHUNK, stack), 0)
    as_ = lax.broadcasted_iota(jnp.int32, (HG_CHUNK, stack), 1) % HG_CHUNK
    causal = (as_ <= at, as_ >= at)
    hq = HG_GROUP // 2
    zq = jnp.zeros((hq, hq), BF16)
    ref_row = (HG_CHUNK // 2 - 1, HG_CHUNK // 2)
    end_row = (HG_CHUNK - 1, 0)

    lb_all = lb_ref[...]

    def qvg(t, carry):
        rows = pl.ds(pl.multiple_of(t * ROW_TILE, ROW_TILE), ROW_TILE)
        ut = u_ref[0, rows, :]
        qv = _dot(ut, w_ref[:, 0:c_f])
        q_s[rows, :] = qv[:, 0:kw]
        v_s[rows, :] = qv[:, kw:c_f].astype(BF16)
        gate_s[rows, :] = _silu(_dot(ut, w_ref[:, c_gate:c_gate + vw])).astype(BF16)
        return carry

    lax.fori_loop(0, n_tiles, qvg, 0)

    def tiles_of(step):
        t_fwd = jnp.where(step < ctx_tiles, lat_tiles + step, step - ctx_tiles)
        t_bwd = n_tiles - 1 - step
        return (t_fwd, t_bwd)

    def precompute(step, slot):
        tl = tiles_of(step)
        for d in range(2):
            idx = slot * 2 + d
            rows = pl.ds(pl.multiple_of(tl[d] * ROW_TILE, ROW_TILE), ROW_TILE)
            zf = _dot(u_ref[0, rows, :], w_ref[:, c_f + d * kw:c_f + (d + 1) * kw])
            lb = lb_all[:, d * kw:(d + 1) * kw]
            e = jnp.exp(-jnp.abs(zf))
            log_sig = jnp.minimum(zf, 0.0) - jnp.log(1.0 + e)
            a = jnp.log(lb)
            c = jnp.log(1.0 - lb) + log_sig
            log_f = jnp.maximum(a, c) + jnp.log(1.0 + jnp.exp(-jnp.abs(a - c)))
            r = 1.0 / (1.0 + e)
            k_s[idx] = (1.0 - lb) * jnp.where(zf >= 0.0, e * r, r)
            g_hi, g_lo = _split2(log_f)
            cum = (_dot(tri[d], g_hi) + _dot(tri[d], g_lo)) * LOG2_E
            b_s[idx] = cum[0:ROW_TILE]
            tot = cum[ROW_TILE:ROW_TILE + n_chunk]
            dt = jnp.concatenate([jnp.exp2(tot), jnp.zeros((V7X_LANES - n_chunk, kw), F32)], axis=0)
            dt_s[idx] = dt.T

    def chunks(step, slot):
        tl = tiles_of(step)
        for cidx in range(n_chunk):
            for d in range(2):
                idx = slot * 2 + d
                cl = cidx if d == 0 else n_chunk - 1 - cidx
                c0 = cl * HG_CHUNK
                rows = slice(c0, c0 + HG_CHUNK)
                grow = pl.ds(pl.multiple_of(tl[d] * ROW_TILE + c0, HG_CHUNK), HG_CHUNK)
                for g in range(n_grp):
                    lanes = slice(g * HG_GROUP, (g + 1) * HG_GROUP)
                    q = q_s[grow, lanes]
                    v = v_s[grow, lanes]
                    k = k_s[idx, rows, lanes]
                    b = b_s[idx, rows, lanes]
                    b_ref = b_s[idx, c0 + ref_row[d]:c0 + ref_row[d] + 1, lanes]
                    b_end = b_s[idx, c0 + end_row[d]:c0 + end_row[d] + 1, lanes]
                    qd = (q * jnp.exp2(b - b_ref)).astype(BF16)
                    qs = (q * jnp.exp2(b)).astype(BF16)
                    kd = (k * jnp.exp2(b_ref - b)).astype(BF16)
                    kr = k * jnp.exp2(b_end - b)
                    kd_st = jnp.concatenate([kd] * heads_per_grp, axis=0) * head_mask
                    v_st = jnp.concatenate([v] * heads_per_grp, axis=0) * head_mask
                    kr_st = jnp.concatenate([kr] * heads_per_grp, axis=0) * head_mask_f
                    kr_t = (kr_st[:, 0:hq] + kr_st[:, hq:HG_GROUP]).T.astype(BF16)
                    amat = jnp.where(causal[d], _dot_nt(qd, kd_st), 0.0).astype(BF16)
                    res = _dot(jnp.concatenate([kr_t, amat], axis=0), v_st)
                    sidx = d * n_grp + g
                    dtc = dt_s[idx, lanes, cl:cl + 1]
                    s0 = st_s[sidx, 0]
                    s1 = st_s[sidx, 1]
                    s_bf = jnp.concatenate(
                        [jnp.concatenate([s0.astype(BF16), zq], axis=1),
                         jnp.concatenate([zq, s1.astype(BF16)], axis=1)], axis=0)
                    o_s[d, grow, lanes] = res[hq:] + _dot(qs, s_bf)
                    st_s[sidx, 0] = s0 * dtc[0:hq] + res[0:hq, 0:hq]
                    st_s[sidx, 1] = s1 * dtc[hq:HG_GROUP] + res[0:hq, hq:HG_GROUP]

    st_s[...] = jnp.zeros_like(st_s)
    precompute(0, 0)

    def step_pair(j, carry):
        i = 2 * j
        precompute(i + 1, 1)
        chunks(i, 0)
        precompute(i + 2, 0)
        chunks(i + 1, 1)
        return carry

    n_pairs = (n_tiles - 1) // 2
    lax.fori_loop(0, n_pairs, step_pair, 0)
    if 2 * n_pairs == n_tiles - 1:
        chunks(n_tiles - 1, 0)
    else:
        precompute(n_tiles - 1, 1)
        chunks(n_tiles - 2, 0)
        chunks(n_tiles - 1, 1)

    hi_ = lax.broadcasted_iota(jnp.int32, (vw, vw), 0) // HG_DV
    hj_ = lax.broadcasted_iota(jnp.int32, (vw, vw), 1) // HG_DV
    head_sum = jnp.where(hi_ == hj_, 1.0, 0.0).astype(BF16)

    def finish(t, carry):
        rows = pl.ds(pl.multiple_of(t * ROW_TILE, ROW_TILE), ROW_TILE)
        o = o_s[0, rows, :] + o_s[1, rows, :]
        sq_hi, sq_lo = _split2(o * o)
        ms = (_dot(sq_hi, head_sum) + _dot(sq_lo, head_sum)) * (1.0 / HG_DV)
        y = o * lax.rsqrt(ms + RMS_EPS) * gn_ref[...]
        out_ref[0, rows, :] = (y * gate_s[rows, :].astype(F32)).astype(BF16)
        return carry

    lax.fori_loop(0, n_out // ROW_TILE, finish, 0)


def _hgrn(u, w_hg, lb, gn, n_lat, n_out):
    bsz, n, d = u.shape
    kw = HG_HEADS * HG_DK
    vw = HG_HEADS * HG_DV
    n_streams = 2 * (kw // HG_GROUP)
    est = (2 * n * d * 2 + 2 * w_hg.size * 2 + 2 * n_out * vw * 2
           + 2 * n * vw * 4 + n_streams * HG_GROUP * HG_GROUP * 2 + n * kw * 4 + n * 2 * vw * 2
           + 8 * ROW_TILE * kw * 4 + 4 * kw * V7X_LANES * 4 + 6 * ROW_TILE * kw * 4)
    full = lambda a: pl.BlockSpec(a.shape, lambda b: (0,) * a.ndim)
    return pl.pallas_call(
        functools.partial(_hgrn_kernel, n_lat=n_lat, n_out=n_out),
        out_shape=jax.ShapeDtypeStruct((bsz, n_out, vw), BF16),
        grid=(bsz,),
        in_specs=[pl.BlockSpec((1, n, d), lambda b: (b, 0, 0)), full(w_hg), full(lb), full(gn)],
        out_specs=pl.BlockSpec((1, n_out, vw), lambda b: (b, 0, 0)),
        scratch_shapes=[
            pltpu.VMEM((2, n, vw), F32),
            pltpu.VMEM((n_streams, 2, HG_GROUP // 2, HG_GROUP // 2), F32),
            pltpu.VMEM((n, kw), F32),
            pltpu.VMEM((n, vw), BF16),
            pltpu.VMEM((n, vw), BF16),
            pltpu.VMEM((4, ROW_TILE, kw), F32),
            pltpu.VMEM((4, ROW_TILE, kw), F32),
            pltpu.VMEM((4, kw, V7X_LANES), F32),
        ],
        compiler_params=pltpu.CompilerParams(
            dimension_semantics=("arbitrary",), vmem_limit_bytes=_vmem_limit(est)),
        name="hgrn_mixer",
    )(u, w_hg, lb, gn)


CV_PAD = 8


def _conv_kernel(u_ref, w_ref, cw_ref, cb_ref, out_ref, uu_s, coef_s, *, n_lat, n_out):
    n = u_ref.shape[1]
    cw = cw_ref.shape[-1]
    out_tiles = n_out // ROW_TILE
    lat_tiles = n_lat // ROW_TILE
    zero = jnp.zeros((CV_PAD, cw), F32)
    uu_s[0:CV_PAD, :] = zero
    uu_s[CV_PAD + n_lat:2 * CV_PAD + n_lat, :] = zero
    uu_s[2 * CV_PAD + n:3 * CV_PAD + n, :] = zero

    def off(t):
        return CV_PAD + t * ROW_TILE + (CV_PAD if t >= lat_tiles else 0)

    for t in range(out_tiles):
        rows = slice(t * ROW_TILE, (t + 1) * ROW_TILE)
        z = _dot(u_ref[0, rows, :], w_ref[...])
        uu_s[off(t):off(t) + ROW_TILE, :] = z[:, 2 * cw:3 * cw] * z[:, 0:cw]
        coef_s[rows, :] = z[:, cw:2 * cw] * _silu(z[:, 3 * cw:4 * cw])
    w0 = cw_ref[0:1, :]
    w1 = cw_ref[1:2, :]
    w2 = cw_ref[2:3, :]
    bias = cb_ref[...]
    for t in range(out_tiles):
        rows = slice(t * ROW_TILE, (t + 1) * ROW_TILE)
        o = off(t)
        conv = (uu_s[o - 1:o - 1 + ROW_TILE, :] * w0 + uu_s[o:o + ROW_TILE, :] * w1
                + uu_s[o + 1:o + 1 + ROW_TILE, :] * w2 + bias)
        out_ref[0, rows, :] = (coef_s[rows, :] * conv).astype(BF16)


def _conv(u, w_cv, conv_w, conv_b, n_lat, n_out):
    bsz, n, d = u.shape
    cw = conv_w.shape[-1]
    est = (2 * n * d * 2 + 2 * w_cv.size * 2 + 2 * n_out * cw * 2 + (n + 3 * CV_PAD) * cw * 4 + n * cw * 4
           + 4 * ROW_TILE * 4 * cw * 4)
    full = lambda a: pl.BlockSpec(a.shape, lambda b: (0,) * a.ndim)
    return pl.pallas_call(
        functools.partial(_conv_kernel, n_lat=n_lat, n_out=n_out),
        out_shape=jax.ShapeDtypeStruct((bsz, n_out, cw), BF16),
        grid=(bsz,),
        in_specs=[pl.BlockSpec((1, n, d), lambda b: (b, 0, 0)), full(w_cv), full(conv_w), full(conv_b)],
        out_specs=pl.BlockSpec((1, n_out, cw), lambda b: (b, 0, 0)),
        scratch_shapes=[pltpu.VMEM((n + 3 * CV_PAD, cw), F32), pltpu.VMEM((n, cw), F32)],
        compiler_params=pltpu.CompilerParams(
            dimension_semantics=("arbitrary",), vmem_limit_bytes=_vmem_limit(est)),
        name="conv_mixer",
    )(u, w_cv, conv_w, conv_b)


def _merge_kernel(u_ref, ym_ref, yh_ref, yc_ref, h_ref, modb_ref, modc_ref, wg_ref, wb_ref, wo_ref,
                  nmodb_ref, nmodc_ref, ng_ref, *out_refs, n_lat, tm, last):
    d = h_ref.shape[-1]
    u = u_ref[0]
    acc = None
    for i, y_ref in enumerate((ym_ref, yh_ref, yc_ref)):
        gate_i = _sigmoid(_dot(u, wg_ref[:, i * d:(i + 1) * d]))
        term = gate_i * _dot(y_ref[0], wb_ref[i])
        acc = term if acc is None else acc + term
    out = _dot(acc.astype(BF16), wo_ref[...])
    row = pl.program_id(1) * tm + lax.broadcasted_iota(jnp.int32, (tm, 1), 0)
    is_lat = row < n_lat
    gate = jnp.where(is_lat, modb_ref[0][:, 2 * d:3 * d], modc_ref[0][:, 2 * d:3 * d])
    hn = h_ref[0] + gate * out
    if last:
        o_ref, = out_refs
        o_ref[0] = hn * lax.rsqrt(jnp.mean(hn * hn, axis=-1, keepdims=True) + RMS_EPS) * ng_ref[...]
    else:
        o_ref, un_ref = out_refs
        o_ref[0] = hn
        un_ref[0] = _mod_norm(hn, ng_ref[...], nmodb_ref[0], nmodc_ref[0], is_lat)


def _merge(u, ym, yh, yc, h, mod3, w_gate, w_branch, w_out, next_mod3, next_g, n_lat, tm, last):
    bsz, n, d = h.shape
    bw = ym.shape[-1]
    n_out = n_lat if last else n
    ctx_row = bsz
    tok = lambda w: pl.BlockSpec((1, tm, w), lambda b, j: (b, j, 0))
    modb = pl.BlockSpec((1, 1, 3 * d), lambda b, j: (b, 0, 0))
    modc = pl.BlockSpec((1, 1, 3 * d), lambda b, j: (ctx_row, 0, 0))
    full = lambda a: pl.BlockSpec(a.shape, lambda b, j: (0,) * a.ndim)
    once = lambda a: pl.BlockSpec(a.shape, lambda b, j: (0,) * a.ndim, pipeline_mode=pl.Buffered(1))
    est = (2 * tm * (d * 2 + 3 * bw * 2 + 2 * d * 4 + d * 2) + (w_gate.size + w_branch.size + w_out.size) * 2
           + 6 * tm * d * 4)
    out_shape = [jax.ShapeDtypeStruct((bsz, n_out, d), F32)]
    out_specs = [tok(d)]
    if not last:
        out_shape.append(jax.ShapeDtypeStruct((bsz, n_out, d), BF16))
        out_specs.append(tok(d))
    return pl.pallas_call(
        functools.partial(_merge_kernel, n_lat=n_lat, tm=tm, last=last),
        out_shape=out_shape,
        grid=(bsz, n_out // tm),
        in_specs=[
            tok(d), tok(bw), tok(bw), tok(bw), tok(d), modb, modc,
            once(w_gate), once(w_branch), once(w_out), modb, modc, full(next_g),
        ],
        out_specs=out_specs,
        compiler_params=pltpu.CompilerParams(
            dimension_semantics=("arbitrary", "arbitrary"), vmem_limit_bytes=_vmem_limit(est)),
        name="merge_out",
    )(u, ym, yh, yc, h, mod3, mod3, w_gate, w_branch, w_out, next_mod3, next_mod3, next_g)


def _rope_tables(n_lat, n_ctx):
    pairs = MLA_ROPE // 4
    rows = n_lat // GRID_W
    row_id = np.repeat(np.arange(rows, dtype=np.float32), GRID_W)
    col_id = np.tile(np.arange(GRID_W, dtype=np.float32), rows)
    inv_freq = jnp.power(ROPE_BASE, -jnp.arange(pairs, dtype=F32) / pairs)
    ang = jnp.stack([row_id[:, None] * inv_freq, col_id[:, None] * inv_freq], axis=1)
    ang = jnp.broadcast_to(ang[:, :, None, :], (n_lat, 2, 2, pairs)).reshape(n_lat, MLA_ROPE)
    pad_hi = MLA_SLOT - MLA_NOPE - MLA_ROPE
    cos = jnp.concatenate([jnp.ones((n_lat, MLA_NOPE), F32), jnp.cos(ang), jnp.ones((n_lat, pad_hi), F32)], axis=1)
    sin = jnp.concatenate([jnp.zeros((n_lat, MLA_NOPE), F32), jnp.sin(ang), jnp.zeros((n_lat, pad_hi), F32)], axis=1)
    cos = jnp.concatenate([cos, jnp.ones((n_ctx, MLA_SLOT), F32)], axis=0)
    sin = jnp.concatenate([sin, jnp.zeros((n_ctx, MLA_SLOT), F32)], axis=0)
    return cos, sin


def _rotate_half_cols(w):
    pairs = MLA_ROPE // 4
    ws = w.reshape(w.shape[:-1] + (2, 2, pairs))
    return jnp.stack([-ws[..., 1, :], ws[..., 0, :]], axis=-2).reshape(w.shape)


def _mla_weights(w_in_l, w_uq_l, w_ukv_l, q_rank, kv_rank):
    d = w_in_l.shape[0]
    width = MLA_HEADS * MLA_V
    pad_hi = MLA_SLOT - MLA_NOPE - MLA_ROPE
    c_kr = q_rank + kv_rank
    w_kr = w_in_l[:, c_kr:c_kr + MLA_ROPE]
    slot = lambda w: jnp.concatenate([jnp.zeros((d, MLA_NOPE), w.dtype), w, jnp.zeros((d, pad_hi), w.dtype)], axis=1)
    w_mla = jnp.concatenate([w_in_l[:, 0:c_kr], slot(w_kr), slot(_rotate_half_cols(w_kr)),
                             w_in_l[:, c_kr + MLA_ROPE:c_kr + MLA_ROPE + width]], axis=1)
    uq = w_uq_l.reshape(q_rank, MLA_HEADS, MLA_NOPE + MLA_ROPE)
    uq_n, uq_r = uq[..., :MLA_NOPE], uq[..., MLA_NOPE:]
    zn = jnp.zeros_like(uq_n)
    zp = jnp.zeros((q_rank, MLA_HEADS, pad_hi), uq.dtype)
    main = jnp.concatenate([uq_n, uq_r, zp], axis=-1).reshape(q_rank, MLA_HEADS * MLA_SLOT)
    part = jnp.concatenate([zn, _rotate_half_cols(uq_r), zp], axis=-1).reshape(q_rank, MLA_HEADS * MLA_SLOT)
    wuq2 = jnp.concatenate([main, part], axis=1)
    ukv = w_ukv_l.reshape(kv_rank, MLA_HEADS, MLA_NOPE + MLA_V)
    kn = ukv[..., :MLA_NOPE]
    wukn = jnp.concatenate([kn, jnp.zeros((kv_rank, MLA_HEADS, MLA_SLOT - MLA_NOPE), kn.dtype)], axis=-1)
    wukn = wukn.reshape(kv_rank, MLA_HEADS * MLA_SLOT)
    wvt = ukv[..., MLA_NOPE:].reshape(kv_rank, width).T
    return w_mla, wuq2, wukn, wvt


def kernel(x, c, ctx, c_ctx, ada_w, ada_b, norm_g, w_in, mla_q_norm_g, mla_kv_norm_g, mla_w_uq, mla_w_ukv,
           hg_lb_logits, hg_norm_g, conv_w, conv_b, w_branch, w_out, final_norm_g):
    bsz, n_lat, d = x.shape
    n_ctx = ctx.shape[1]
    n = n_lat + n_ctx
    depth = w_in.shape[0]
    q_rank = mla_q_norm_g.shape[-1]
    kv_rank = mla_kv_norm_g.shape[-1]
    width = MLA_HEADS * MLA_V
    kw = HG_HEADS * HG_DK
    vw = HG_HEADS * HG_DV
    cw = conv_w.shape[-1]
    assert n_lat % ROW_TILE == 0 and n_ctx % ROW_TILE == 0 and n_lat % GRID_W == 0
    assert width == vw == cw == w_branch.shape[2]

    pad = (-(bsz + 1)) % 8
    cs = jnp.concatenate([c, c_ctx[None, :], jnp.zeros((pad, d), c.dtype)], axis=0).astype(F32)
    mod = _modulation(cs, ada_w.astype(F32), ada_b.astype(F32))
    mod3 = [mod[l].reshape(mod.shape[1], 1, 3 * d) for l in range(depth)]
    lb_all = _lower_bounds(hg_lb_logits)
    cos_t, sin_t = _rope_tables(n_lat, n_ctx)

    c_hg = q_rank + kv_rank + MLA_ROPE + width
    c_cv = c_hg + 2 * kw + vw + kw + vw
    c_gate = c_cv + 4 * cw
    w_in_b = w_in.astype(BF16)
    w_branch_b = w_branch.astype(BF16)
    w_out_b = w_out.astype(BF16)
    norm_gs = [norm_g[l].reshape(1, d).astype(F32) for l in range(depth)]
    final_g = final_norm_g.reshape(1, d).astype(F32)

    h = jnp.concatenate([x, ctx], axis=1).astype(F32)
    tm_merge = 768 if n % 768 == 0 else ROW_TILE
    tm_last = 1024 if n_lat % 1024 == 0 else ROW_TILE
    u = _norm_mod(h, mod3[0], norm_gs[0], n_lat, ROW_TILE)
    for l in range(depth):
        last = l == depth - 1
        n_out = n_lat if last else n
        w_mla, wuq2, wukn, wvt = _mla_weights(w_in_b[l], mla_w_uq[l].astype(BF16), mla_w_ukv[l].astype(BF16),
                                              q_rank, kv_rank)
        y_mla = _mla(u, w_mla, mla_q_norm_g[l].reshape(1, q_rank).astype(F32),
                     mla_kv_norm_g[l].reshape(1, kv_rank).astype(F32), wuq2, wukn, wvt, cos_t, sin_t, n_lat, n_out)
        y_hg = _hgrn(u, w_in_b[l, :, c_hg:c_cv], lb_all[l].reshape(1, 2 * kw),
                     hg_norm_g[l].reshape(1, vw).astype(F32), n_lat, n_out)
        y_cv = _conv(u, w_in_b[l, :, c_cv:c_gate], conv_w[l].astype(F32), conv_b[l].reshape(1, cw).astype(F32),
                     n_lat, n_out)
        if last:
            h, = _merge(u, y_mla, y_hg, y_cv, h, mod3[l], w_in_b[l, :, c_gate:], w_branch_b[l], w_out_b[l],
                        mod3[l], final_g, n_lat, tm_last, True)
        else:
            h, u = _merge(u, y_mla, y_hg, y_cv, h, mod3[l], w_in_b[l, :, c_gate:], w_branch_b[l], w_out_b[l],
                          mod3[l + 1], norm_gs[l + 1], n_lat, tm_merge, False)
    return h
```

```python
import functools

import numpy as np
import jax
import jax.numpy as jnp
from jax import lax
from jax.experimental import pallas as pl
from jax.experimental.pallas import tpu as pltpu

F32 = jnp.float32
BF16 = jnp.bfloat16

RMS_EPS = 1e-6
GRID_W = 64
ROPE_BASE = 10000.0
MLA_HEADS = 8
MLA_NOPE = 64
MLA_ROPE = 32
MLA_V = 64
MLA_SLOT = 128
MLA_KEY_BLOCK = 768
MLA_SCALE = (MLA_NOPE + MLA_ROPE) ** -0.5
LOG2_E = float(np.log2(np.e))
MLA_SCALE_LOG2 = MLA_SCALE * LOG2_E
HG_HEADS = 8
HG_DK = 64
HG_DV = 64
HG_GROUP = 256
HG_CHUNK = 32
CV_K = 3

V7X_VMEM_BYTES = 64 * 1024 * 1024
V7X_LANES = 128
ROW_TILE = 256


def _vmem_limit(nbytes):
    return int(min(nbytes + (12 << 20), V7X_VMEM_BYTES - (6 << 20)))


def _dot(a, b):
    return jnp.dot(a, b, preferred_element_type=F32)


def _dot_nt(a, b):
    return lax.dot_general(a, b, (((1,), (1,)), ((), ())), preferred_element_type=F32)


def _sigmoid(x):
    return 1.0 / (1.0 + jnp.exp(-x))


def _silu(x):
    return x * _sigmoid(x)


def _split2(x):
    hi = x.astype(BF16)
    lo = (x - hi.astype(F32)).astype(BF16)
    return hi, lo


def _mod_norm(x, g, mod_b, mod_c, is_lat):
    d = x.shape[-1]
    y = x * lax.rsqrt(jnp.mean(x * x, axis=-1, keepdims=True) + RMS_EPS) * g
    shift = jnp.where(is_lat, mod_b[:, 0:d], mod_c[:, 0:d])
    scale = jnp.where(is_lat, mod_b[:, d:2 * d], mod_c[:, d:2 * d])
    return (y * (1.0 + scale) + shift).astype(BF16)


def _mod_kernel(c_ref, w_ref, b_ref, o_ref):
    s = _silu(c_ref[...])
    o_ref[0] = jnp.dot(s, w_ref[0], preferred_element_type=F32, precision=lax.Precision.HIGHEST) + b_ref[0]


def _modulation(cs, ada_w, ada_b):
    depth, d, d3 = ada_w.shape
    rows = cs.shape[0]
    nb = d3 // d
    return pl.pallas_call(
        _mod_kernel,
        out_shape=jax.ShapeDtypeStruct((depth, rows, d3), F32),
        grid=(depth, nb),
        in_specs=[
            pl.BlockSpec((rows, d), lambda l, j: (0, 0)),
            pl.BlockSpec((1, d, d), lambda l, j: (l, 0, j)),
            pl.BlockSpec((1, 1, d), lambda l, j: (l, 0, j)),
        ],
        out_specs=pl.BlockSpec((1, rows, d), lambda l, j: (l, 0, j)),
        compiler_params=pltpu.CompilerParams(
            dimension_semantics=("arbitrary", "arbitrary"),
            vmem_limit_bytes=_vmem_limit(2 * d * d * 4 + 4 * rows * d * 4),
        ),
        name="adaln_mod",
    )(cs, ada_w, ada_b.reshape(depth, 1, d3))


def _lb_kernel(x_ref, o_ref):
    x = x_ref[...]
    m = jnp.max(x, axis=0, keepdims=True)
    e = jnp.exp(x - m)
    p = e / jnp.sum(e, axis=0, keepdims=True)
    depth = x.shape[0]
    rows = [p[0:1]]
    for l in range(1, depth):
        rows.append(rows[-1] + p[l:l + 1])
    cs = jnp.concatenate(rows, axis=0)
    o_ref[...] = cs - cs[0:1]


def _lower_bounds(hg_lb_logits):
    depth = hg_lb_logits.shape[0]
    flat = hg_lb_logits.reshape(depth, -1).astype(F32)
    return pl.pallas_call(
        _lb_kernel,
        out_shape=jax.ShapeDtypeStruct(flat.shape, F32),
        name="hgrn_lower_bounds",
    )(flat)


def _norm_kernel(h_ref, modb_ref, modc_ref, g_ref, u_ref, *, n_lat, tm):
    row = pl.program_id(1) * tm + lax.broadcasted_iota(jnp.int32, (tm, 1), 0)
    u_ref[0] = _mod_norm(h_ref[0], g_ref[...], modb_ref[0], modc_ref[0], row < n_lat)


def _norm_mod(h, mod3, norm_g, n_lat, tm):
    bsz, n, d = h.shape
    ctx_row = bsz
    return pl.pallas_call(
        functools.partial(_norm_kernel, n_lat=n_lat, tm=tm),
        out_shape=jax.ShapeDtypeStruct((bsz, n, d), BF16),
        grid=(bsz, n // tm),
        in_specs=[
            pl.BlockSpec((1, tm, d), lambda b, j: (b, j, 0)),
            pl.BlockSpec((1, 1, 3 * d), lambda b, j: (b, 0, 0)),
            pl.BlockSpec((1, 1, 3 * d), lambda b, j: (ctx_row, 0, 0)),
            pl.BlockSpec((1, d), lambda b, j: (0, 0)),
        ],
        out_specs=pl.BlockSpec((1, tm, d), lambda b, j: (b, j, 0)),
        compiler_params=pltpu.CompilerParams(
            dimension_semantics=("arbitrary", "arbitrary"),
            vmem_limit_bytes=_vmem_limit(2 * tm * d * 6 + 4 * tm * d * 4),
        ),
        name="norm_mod",
    )(h, mod3, mod3, norm_g.reshape(1, d))


def _mla_kernel(u_ref, w_ref, gq_ref, gkv_ref, wuq_ref, wukn_ref, wvt_ref, cos_ref, sin_ref, out_ref,
                q_s, k_s, vt_s, g_s, ot_s, st_s, *, n_lat, n_out, q_rank, kv_rank):
    n = u_ref.shape[1]
    n_ctx = n - n_lat
    hs = MLA_HEADS * MLA_SLOT
    width = MLA_HEADS * MLA_V
    c_kv = q_rank
    c_kr = q_rank + kv_rank
    c_krs = c_kr + MLA_SLOT
    c_g = c_krs + MLA_SLOT

    for t in range(n // ROW_TILE):
        rows = slice(t * ROW_TILE, (t + 1) * ROW_TILE)
        z = _dot(u_ref[0, rows, :], w_ref[...])
        cos_t = cos_ref[rows, :]
        sin_t = sin_ref[rows, :]
        cos8 = jnp.concatenate([cos_t] * MLA_HEADS, axis=1)
        sin8 = jnp.concatenate([sin_t] * MLA_HEADS, axis=1)
        ckv = z[:, c_kv:c_kr]
        kvn = (ckv * lax.rsqrt(jnp.mean(ckv * ckv, axis=-1, keepdims=True) + RMS_EPS) * gkv_ref[...]).astype(BF16)
        kn = _dot(kvn, wukn_ref[...])
        kr = z[:, c_kr:c_krs] * cos_t + z[:, c_krs:c_g] * sin_t
        k = (kn + jnp.concatenate([kr] * MLA_HEADS, axis=1)).astype(BF16)
        for h in range(MLA_HEADS):
            k_s[h, rows, :] = k[:, h * MLA_SLOT:(h + 1) * MLA_SLOT]
        vt = _dot_nt(wvt_ref[...], kvn).astype(BF16)
        for h in range(MLA_HEADS):
            vt_s[h, :, rows] = vt[h * MLA_V:(h + 1) * MLA_V, :]
        if t * ROW_TILE < n_out:
            cq = z[:, 0:q_rank]
            cqn = (cq * lax.rsqrt(jnp.mean(cq * cq, axis=-1, keepdims=True) + RMS_EPS) * gq_ref[...]).astype(BF16)
            q2 = _dot(cqn, wuq_ref[...])
            q = ((q2[:, 0:hs] * cos8 + q2[:, hs:2 * hs] * sin8) * MLA_SCALE_LOG2).astype(BF16)
            for h in range(MLA_HEADS):
                q_s[h, rows, :] = q[:, h * MLA_SLOT:(h + 1) * MLA_SLOT]
            g_s[rows, :] = _silu(z[:, c_g:c_g + width])

    def key_block(nk):
        return MLA_KEY_BLOCK if nk % MLA_KEY_BLOCK == 0 else nk

    def scores_block(h, r0, k0, kb, kbs, m):
        st = _dot_nt(k_s[h, k0 + kb:k0 + kb + kbs, :], q_s[h, pl.ds(r0, ROW_TILE), :])
        st_s[h % 2, kb:kb + kbs, :] = st
        bm = jnp.max(st, axis=0, keepdims=True)
        return bm if m is None else jnp.maximum(m, bm)

    def scores(h, r0, k0, nk):
        m = None
        kbs = key_block(nk)
        for kb in range(0, nk, kbs):
            m = scores_block(h, r0, k0, kb, kbs, m)
        return m

    def attend(r0, k0, nk, m0, r0_next):
        rows = pl.ds(r0, ROW_TILE)
        kbs = key_block(nk)
        m_next = m0
        for h in range(MLA_HEADS):
            m = m_next
            m_next = None
            l = None
            ot = None
            for kb in range(0, nk, kbs):
                if h + 1 < MLA_HEADS:
                    m_next = scores_block(h + 1, r0, k0, kb, kbs, m_next)
                elif r0_next is not None:
                    m_next = scores_block(0, r0_next, k0, kb, kbs, m_next)
                p = jnp.exp2(st_s[h % 2, kb:kb + kbs, :] - m)
                bl = jnp.sum(p, axis=0, keepdims=True)
                l = bl if l is None else l + bl
                bo = _dot(vt_s[h, :, k0 + kb:k0 + kb + kbs], p.astype(BF16))
                ot = bo if ot is None else ot + bo
            ot_s[h] = ot * (1.0 / l)
        o = jnp.concatenate([ot_s[h] for h in range(MLA_HEADS)], axis=0).T
        out_ref[0, rows, :] = (o * g_s[rows, :]).astype(BF16)
        return m_next

    lat_tiles = n_lat // ROW_TILE

    def lat_tile(t, m0):
        r0 = pl.multiple_of(t * ROW_TILE, ROW_TILE)
        r0_next = pl.multiple_of(jnp.minimum(t + 1, lat_tiles - 1) * ROW_TILE, ROW_TILE)
        return attend(r0, 0, n, m0, r0_next)

    lax.fori_loop(0, lat_tiles, lat_tile, scores(0, 0, 0, n))
    for t in range((n_out - n_lat) // ROW_TILE):
        r0 = n_lat + t * ROW_TILE
        attend(r0, n_lat, n_ctx, scores(0, r0, n_lat, n_ctx), None)


def _mla(u, w_mla, gq, gkv, wuq2, wukn, wvt, cos_t, sin_t, n_lat, n_out):
    bsz, n, d = u.shape
    q_rank, kv_rank = gq.shape[-1], gkv.shape[-1]
    width = MLA_HEADS * MLA_V
    hs = MLA_HEADS * MLA_SLOT
    est = (2 * n * d * 2 + 2 * w_mla.size * 2 + 2 * (wuq2.size + wukn.size + wvt.size) * 2
           + 4 * n * MLA_SLOT * 4 + 2 * n_out * width * 2
           + 2 * n * hs * 2 + n * width * 2 + n * width * 4 + width * ROW_TILE * 4
           + 2 * n * ROW_TILE * 4 + 2 * n * ROW_TILE * 4)
    full = lambda a: pl.BlockSpec(a.shape, lambda b: (0,) * a.ndim)
    return pl.pallas_call(
        functools.partial(_mla_kernel, n_lat=n_lat, n_out=n_out, q_rank=q_rank, kv_rank=kv_rank),
        out_shape=jax.ShapeDtypeStruct((bsz, n_out, width), BF16),
        grid=(bsz,),
        in_specs=[
            pl.BlockSpec((1, n, d), lambda b: (b, 0, 0)),
            full(w_mla), full(gq), full(gkv), full(wuq2), full(wukn), full(wvt), full(cos_t), full(sin_t),
        ],
        out_specs=pl.BlockSpec((1, n_out, width), lambda b: (b, 0, 0)),
        scratch_shapes=[
            pltpu.VMEM((MLA_HEADS, n, MLA_SLOT), BF16),
            pltpu.VMEM((MLA_HEADS, n, MLA_SLOT), BF16),
            pltpu.VMEM((MLA_HEADS, MLA_V, n), BF16),
            pltpu.VMEM((n, width), F32),
            pltpu.VMEM((MLA_HEADS, MLA_V, ROW_TILE), F32),
            pltpu.VMEM((2, n, ROW_TILE), F32),
        ],
        compiler_params=pltpu.CompilerParams(
            dimension_semantics=("arbitrary",), vmem_limit_bytes=_vmem_limit(est)),
        name="mla_mixer",
    )(u, w_mla, gq, gkv, wuq2, wukn, wvt, cos_t, sin_t)


def _hgrn_kernel(u_ref, w_ref, lb_ref, gn_ref, out_ref,
                 o_s, st_s, q_s, v_s, gate_s, k_s, b_s, dt_s, zf_s, tot_s, *, n_lat, n_out):
    n = u_ref.shape[1]
    kw = HG_HEADS * HG_DK
    vw = HG_HEADS * HG_DV
    c_f = kw + vw
    c_gate = c_f + 2 * kw
    n_tiles = n // ROW_TILE
    lat_tiles = n_lat // ROW_TILE
    ctx_tiles = n_tiles - lat_tiles
    n_chunk = ROW_TILE // HG_CHUNK
    n_grp = kw // HG_GROUP
    heads_per_grp = HG_GROUP // HG_DK
    stack = heads_per_grp * HG_CHUNK

    ti = lax.broadcasted_iota(jnp.int32, (HG_CHUNK + 16, HG_CHUNK), 0)
    si = lax.broadcasted_iota(jnp.int32, (HG_CHUNK + 16, HG_CHUNK), 1)
    tri = tuple(jnp.where(((ti < HG_CHUNK) & m) | (ti == HG_CHUNK), 1.0, 0.0).astype(BF16)
                for m in (si <= ti, si >= ti))
    hr = lax.broadcasted_iota(jnp.int32, (stack, HG_GROUP), 0) // HG_CHUNK
    hc = lax.broadcasted_iota(jnp.int32, (stack, HG_GROUP), 1) // HG_DK
    head_mask_f = jnp.where(hr == hc, 1.0, 0.0)
    head_mask = head_mask_f.astype(BF16)
    at = lax.broadcasted_iota(jnp.int32, (HG_CHUNK, stack), 0)
    as_ = lax.broadcasted_iota(jnp.int32, (HG_CHUNK, stack), 1) % HG_CHUNK
    causal = (as_ <= at, as_ >= at)
    hq = HG_GROUP // 2
    zq = jnp.zeros((hq, hq), BF16)
    ref_row = (HG_CHUNK // 2 - 1, HG_CHUNK // 2)
    end_row = (HG_CHUNK - 1, 0)

    lb_all = lb_ref[...]

    def qvg(t, carry):
        rows = pl.ds(pl.multiple_of(t * ROW_TILE, ROW_TILE), ROW_TILE)
        ut = u_ref[0, rows, :]
        qv = _dot(ut, w_ref[:, 0:c_f])
        q_s[rows, :] = qv[:, 0:kw]
        v_s[rows, :] = qv[:, kw:c_f].astype(BF16)
        gate_s[rows, :] = _silu(_dot(ut, w_ref[:, c_gate:c_gate + vw])).astype(BF16)
        return carry

    lax.fori_loop(0, n_tiles, qvg, 0)

    def tiles_of(step):
        t_fwd = jnp.where(step < ctx_tiles, lat_tiles + step, step - ctx_tiles)
        t_bwd = n_tiles - 1 - step
        return (t_fwd, t_bwd)

    def precompute(step, slot):
        tl = tiles_of(step)

        def logits():
            for d in range(2):
                rows = pl.ds(pl.multiple_of(tl[d] * ROW_TILE, ROW_TILE), ROW_TILE)
                zf_s[d] = _dot(u_ref[0, rows, :], w_ref[:, c_f + d * kw:c_f + (d + 1) * kw])

        def gates(cl):
            r = slice(cl * HG_CHUNK, (cl + 1) * HG_CHUNK)
            for d in range(2):
                idx = slot * 2 + d
                zf = zf_s[d, r, :]
                lb = lb_all[:, d * kw:(d + 1) * kw]
                e = jnp.exp(-jnp.abs(zf))
                log_sig = jnp.minimum(zf, 0.0) - jnp.log(1.0 + e)
                a = jnp.log(lb)
                c = jnp.log(1.0 - lb) + log_sig
                log_f = jnp.maximum(a, c) + jnp.log(1.0 + jnp.exp(-jnp.abs(a - c)))
                rcp = 1.0 / (1.0 + e)
                k_s[idx, r, :] = (1.0 - lb) * jnp.where(zf >= 0.0, e * rcp, rcp)
                g_hi, g_lo = _split2(log_f)
                cum = (_dot(tri[d], g_hi) + _dot(tri[d], g_lo)) * LOG2_E
                b_s[idx, r, :] = cum[0:HG_CHUNK]
                tot_s[d, cl:cl + 1, :] = cum[HG_CHUNK:HG_CHUNK + 1]

        def decays():
            for d in range(2):
                dt_s[slot * 2 + d] = jnp.exp2(tot_s[d]).T

        return [logits] + [functools.partial(gates, cl) for cl in range(n_chunk)] + [decays]

    def chunks(step, slot, between=()):
        tl = tiles_of(step)
        between = list(between)
        for cidx in range(n_chunk):
            for d in range(2):
                idx = slot * 2 + d
                cl = cidx if d == 0 else n_chunk - 1 - cidx
                c0 = cl * HG_CHUNK
                rows = slice(c0, c0 + HG_CHUNK)
                grow = pl.ds(pl.multiple_of(tl[d] * ROW_TILE + c0, HG_CHUNK), HG_CHUNK)
                for g in range(n_grp):
                    lanes = slice(g * HG_GROUP, (g + 1) * HG_GROUP)
                    q = q_s[grow, lanes]
                    v = v_s[grow, lanes]
                    k = k_s[idx, rows, lanes]
                    b = b_s[idx, rows, lanes]
                    b_ref = b_s[idx, c0 + ref_row[d]:c0 + ref_row[d] + 1, lanes]
                    b_end = b_s[idx, c0 + end_row[d]:c0 + end_row[d] + 1, lanes]
                    qd = (q * jnp.exp2(b - b_ref)).astype(BF16)
                    qs = (q * jnp.exp2(b)).astype(BF16)
                    kd = (k * jnp.exp2(b_ref - b)).astype(BF16)
                    kr = k * jnp.exp2(b_end - b)
                    kd_st = jnp.concatenate([kd] * heads_per_grp, axis=0) * head_mask
                    v_st = jnp.concatenate([v] * heads_per_grp, axis=0) * head_mask
                    kr_st = jnp.concatenate([kr] * heads_per_grp, axis=0) * head_mask_f
                    kr_t = (kr_st[:, 0:hq] + kr_st[:, hq:HG_GROUP]).T.astype(BF16)
                    amat = jnp.where(causal[d], _dot_nt(qd, kd_st), 0.0).astype(BF16)
                    res = _dot(jnp.concatenate([kr_t, amat], axis=0), v_st)
                    sidx = d * n_grp + g
                    dtc = dt_s[idx, lanes, cl:cl + 1]
                    s0 = st_s[sidx, 0]
                    s1 = st_s[sidx, 1]
                    s_bf = jnp.concatenate(
                        [jnp.concatenate([s0.astype(BF16), zq], axis=1),
                         jnp.concatenate([zq, s1.astype(BF16)], axis=1)], axis=0)
                    o_s[d, grow, lanes] = res[hq:] + _dot(qs, s_bf)
                    st_s[sidx, 0] = s0 * dtc[0:hq] + res[0:hq, 0:hq]
                    st_s[sidx, 1] = s1 * dtc[hq:HG_GROUP] + res[0:hq, hq:HG_GROUP]
            if between:
                between.pop(0)()
        for piece in between:
            piece()

    def step(i, slot, has_next):
        if has_next:
            pieces = precompute(i + 1, 1 - slot)
            pieces[0]()
            chunks(i, slot, pieces[1:])
        else:
            chunks(i, slot)

    st_s[...] = jnp.zeros_like(st_s)
    tot_s[...] = jnp.zeros_like(tot_s)
    for piece in precompute(0, 0):
        piece()

    def step_pair(j, carry):
        step(2 * j, 0, True)
        step(2 * j + 1, 1, True)
        return carry

    n_pairs = (n_tiles - 1) // 2
    lax.fori_loop(0, n_pairs, step_pair, 0)
    if 2 * n_pairs == n_tiles - 1:
        step(n_tiles - 1, 0, False)
    else:
        step(n_tiles - 2, 0, True)
        step(n_tiles - 1, 1, False)

    hi_ = lax.broadcasted_iota(jnp.int32, (vw, vw), 0) // HG_DV
    hj_ = lax.broadcasted_iota(jnp.int32, (vw, vw), 1) // HG_DV
    head_sum = jnp.where(hi_ == hj_, 1.0, 0.0).astype(BF16)

    def finish(t, carry):
        rows = pl.ds(pl.multiple_of(t * ROW_TILE, ROW_TILE), ROW_TILE)
        o = o_s[0, rows, :] + o_s[1, rows, :]
        sq_hi, sq_lo = _split2(o * o)
        ms = (_dot(sq_hi, head_sum) + _dot(sq_lo, head_sum)) * (1.0 / HG_DV)
        y = o * lax.rsqrt(ms + RMS_EPS) * gn_ref[...]
        out_ref[0, rows, :] = (y * gate_s[rows, :].astype(F32)).astype(BF16)
        return carry

    lax.fori_loop(0, n_out // ROW_TILE, finish, 0)


def _hgrn(u, w_hg, lb, gn, n_lat, n_out):
    bsz, n, d = u.shape
    kw = HG_HEADS * HG_DK
    vw = HG_HEADS * HG_DV
    n_streams = 2 * (kw // HG_GROUP)
    est = (2 * n * d * 2 + 2 * w_hg.size * 2 + 2 * n_out * vw * 2
           + 2 * n * vw * 4 + n_streams * HG_GROUP * HG_GROUP * 2 + n * kw * 4 + n * 2 * vw * 2
           + 8 * ROW_TILE * kw * 4 + 4 * kw * V7X_LANES * 4 + 6 * ROW_TILE * kw * 4)
    full = lambda a: pl.BlockSpec(a.shape, lambda b: (0,) * a.ndim)
    return pl.pallas_call(
        functools.partial(_hgrn_kernel, n_lat=n_lat, n_out=n_out),
        out_shape=jax.ShapeDtypeStruct((bsz, n_out, vw), BF16),
        grid=(bsz,),
        in_specs=[pl.BlockSpec((1, n, d), lambda b: (b, 0, 0)), full(w_hg), full(lb), full(gn)],
        out_specs=pl.BlockSpec((1, n_out, vw), lambda b: (b, 0, 0)),
        scratch_shapes=[
            pltpu.VMEM((2, n, vw), F32),
            pltpu.VMEM((n_streams, 2, HG_GROUP // 2, HG_GROUP // 2), F32),
            pltpu.VMEM((n, kw), F32),
            pltpu.VMEM((n, vw), BF16),
            pltpu.VMEM((n, vw), BF16),
            pltpu.VMEM((4, ROW_TILE, kw), F32),
            pltpu.VMEM((4, ROW_TILE, kw), F32),
            pltpu.VMEM((4, kw, V7X_LANES), F32),
            pltpu.VMEM((2, ROW_TILE, kw), F32),
            pltpu.VMEM((2, V7X_LANES, kw), F32),
        ],
        compiler_params=pltpu.CompilerParams(
            dimension_semantics=("arbitrary",), vmem_limit_bytes=_vmem_limit(est)),
        name="hgrn_mixer",
    )(u, w_hg, lb, gn)


CV_PAD = 8


def _conv_kernel(u_ref, w_ref, cw_ref, cb_ref, out_ref, uu_s, coef_s, *, n_lat, n_out):
    n = u_ref.shape[1]
    cw = cw_ref.shape[-1]
    out_tiles = n_out // ROW_TILE
    lat_tiles = n_lat // ROW_TILE
    zero = jnp.zeros((CV_PAD, cw), F32)
    uu_s[0:CV_PAD, :] = zero
    uu_s[CV_PAD + n_lat:2 * CV_PAD + n_lat, :] = zero
    uu_s[2 * CV_PAD + n:3 * CV_PAD + n, :] = zero

    def off(t):
        return CV_PAD + t * ROW_TILE + (CV_PAD if t >= lat_tiles else 0)

    w0 = cw_ref[0:1, :]
    w1 = cw_ref[1:2, :]
    w2 = cw_ref[2:3, :]
    bias = cb_ref[...]

    def project(t):
        rows = slice(t * ROW_TILE, (t + 1) * ROW_TILE)
        z = _dot(u_ref[0, rows, :], w_ref[...])
        uu_s[off(t):off(t) + ROW_TILE, :] = z[:, 2 * cw:3 * cw] * z[:, 0:cw]
        coef_s[rows, :] = z[:, cw:2 * cw] * _silu(z[:, 3 * cw:4 * cw])

    def emit(t):
        rows = slice(t * ROW_TILE, (t + 1) * ROW_TILE)
        o = off(t)
        conv = (uu_s[o - 1:o - 1 + ROW_TILE, :] * w0 + uu_s[o:o + ROW_TILE, :] * w1
                + uu_s[o + 1:o + 1 + ROW_TILE, :] * w2 + bias)
        out_ref[0, rows, :] = (coef_s[rows, :] * conv).astype(BF16)

    for t in range(out_tiles):
        project(t)
        if t >= 1:
            emit(t - 1)
    emit(out_tiles - 1)


def _conv(u, w_cv, conv_w, conv_b, n_lat, n_out):
    bsz, n, d = u.shape
    cw = conv_w.shape[-1]
    est = (2 * n * d * 2 + 2 * w_cv.size * 2 + 2 * n_out * cw * 2 + (n + 3 * CV_PAD) * cw * 4 + n * cw * 4
           + 4 * ROW_TILE * 4 * cw * 4)
    full = lambda a: pl.BlockSpec(a.shape, lambda b: (0,) * a.ndim)
    return pl.pallas_call(
        functools.partial(_conv_kernel, n_lat=n_lat, n_out=n_out),
        out_shape=jax.ShapeDtypeStruct((bsz, n_out, cw), BF16),
        grid=(bsz,),
        in_specs=[pl.BlockSpec((1, n, d), lambda b: (b, 0, 0)), full(w_cv), full(conv_w), full(conv_b)],
        out_specs=pl.BlockSpec((1, n_out, cw), lambda b: (b, 0, 0)),
        scratch_shapes=[pltpu.VMEM((n + 3 * CV_PAD, cw), F32), pltpu.VMEM((n, cw), F32)],
        compiler_params=pltpu.CompilerParams(
            dimension_semantics=("arbitrary",), vmem_limit_bytes=_vmem_limit(est)),
        name="conv_mixer",
    )(u, w_cv, conv_w, conv_b)


def _merge_kernel(u_ref, ym_ref, yh_ref, yc_ref, h_ref, modb_ref, modc_ref, wg_ref, wb_ref, wo_ref,
                  nmodb_ref, nmodc_ref, ng_ref, *out_refs, n_lat, tm, last):
    d = h_ref.shape[-1]
    sub = ROW_TILE if tm % ROW_TILE == 0 else tm

    def project(s):
        rows = slice(s * sub, (s + 1) * sub)
        u = u_ref[0, rows, :]
        acc = None
        for i, y_ref in enumerate((ym_ref, yh_ref, yc_ref)):
            gate_i = _sigmoid(_dot(u, wg_ref[:, i * d:(i + 1) * d]))
            term = gate_i * _dot(y_ref[0, rows, :], wb_ref[i])
            acc = term if acc is None else acc + term
        return _dot(acc.astype(BF16), wo_ref[...])

    def finish(s, out):
        rows = slice(s * sub, (s + 1) * sub)
        row = pl.program_id(1) * tm + s * sub + lax.broadcasted_iota(jnp.int32, (sub, 1), 0)
        is_lat = row < n_lat
        gate = jnp.where(is_lat, modb_ref[0][:, 2 * d:3 * d], modc_ref[0][:, 2 * d:3 * d])
        hn = h_ref[0, rows, :] + gate * out
        if last:
            o_ref, = out_refs
            o_ref[0, rows, :] = hn * lax.rsqrt(jnp.mean(hn * hn, axis=-1, keepdims=True) + RMS_EPS) * ng_ref[...]
        else:
            o_ref, un_ref = out_refs
            o_ref[0, rows, :] = hn
            un_ref[0, rows, :] = _mod_norm(hn, ng_ref[...], nmodb_ref[0], nmodc_ref[0], is_lat)

    pending = None
    for s in range(tm // sub):
        out = project(s)
        if pending is not None:
            finish(*pending)
        pending = (s, out)
    finish(*pending)


def _merge(u, ym, yh, yc, h, mod3, w_gate, w_branch, w_out, next_mod3, next_g, n_lat, tm, last):
    bsz, n, d = h.shape
    bw = ym.shape[-1]
    n_out = n_lat if last else n
    ctx_row = bsz
    tok = lambda w: pl.BlockSpec((1, tm, w), lambda b, j: (b, j, 0))
    modb = pl.BlockSpec((1, 1, 3 * d), lambda b, j: (b, 0, 0))
    modc = pl.BlockSpec((1, 1, 3 * d), lambda b, j: (ctx_row, 0, 0))
    full = lambda a: pl.BlockSpec(a.shape, lambda b, j: (0,) * a.ndim)
    once = lambda a: pl.BlockSpec(a.shape, lambda b, j: (0,) * a.ndim, pipeline_mode=pl.Buffered(1))
    est = (2 * tm * (d * 2 + 3 * bw * 2 + 2 * d * 4 + d * 2) + (w_gate.size + w_branch.size + w_out.size) * 2
           + 6 * tm * d * 4)
    out_shape = [jax.ShapeDtypeStruct((bsz, n_out, d), F32)]
    out_specs = [tok(d)]
    if not last:
        out_shape.append(jax.ShapeDtypeStruct((bsz, n_out, d), BF16))
        out_specs.append(tok(d))
    return pl.pallas_call(
        functools.partial(_merge_kernel, n_lat=n_lat, tm=tm, last=last),
        out_shape=out_shape,
        grid=(bsz, n_out // tm),
        in_specs=[
            tok(d), tok(bw), tok(bw), tok(bw), tok(d), modb, modc,
            once(w_gate), once(w_branch), once(w_out), modb, modc, full(next_g),
        ],
        out_specs=out_specs,
        compiler_params=pltpu.CompilerParams(
            dimension_semantics=("arbitrary", "arbitrary"), vmem_limit_bytes=_vmem_limit(est)),
        name="merge_out",
    )(u, ym, yh, yc, h, mod3, mod3, w_gate, w_branch, w_out, next_mod3, next_mod3, next_g)


def _rope_tables(n_lat, n_ctx):
    pairs = MLA_ROPE // 4
    rows = n_lat // GRID_W
    row_id = np.repeat(np.arange(rows, dtype=np.float32), GRID_W)
    col_id = np.tile(np.arange(GRID_W, dtype=np.float32), rows)
    inv_freq = jnp.power(ROPE_BASE, -jnp.arange(pairs, dtype=F32) / pairs)
    ang = jnp.stack([row_id[:, None] * inv_freq, col_id[:, None] * inv_freq], axis=1)
    ang = jnp.broadcast_to(ang[:, :, None, :], (n_lat, 2, 2, pairs)).reshape(n_lat, MLA_ROPE)
    pad_hi = MLA_SLOT - MLA_NOPE - MLA_ROPE
    cos = jnp.concatenate([jnp.ones((n_lat, MLA_NOPE), F32), jnp.cos(ang), jnp.ones((n_lat, pad_hi), F32)], axis=1)
    sin = jnp.concatenate([jnp.zeros((n_lat, MLA_NOPE), F32), jnp.sin(ang), jnp.zeros((n_lat, pad_hi), F32)], axis=1)
    cos = jnp.concatenate([cos, jnp.ones((n_ctx, MLA_SLOT), F32)], axis=0)
    sin = jnp.concatenate([sin, jnp.zeros((n_ctx, MLA_SLOT), F32)], axis=0)
    return cos, sin


def _rotate_half_cols(w):
    pairs = MLA_ROPE // 4
    ws = w.reshape(w.shape[:-1] + (2, 2, pairs))
    return jnp.stack([-ws[..., 1, :], ws[..., 0, :]], axis=-2).reshape(w.shape)


def _mla_weights(w_in_l, w_uq_l, w_ukv_l, q_rank, kv_rank):
    d = w_in_l.shape[0]
    width = MLA_HEADS * MLA_V
    pad_hi = MLA_SLOT - MLA_NOPE - MLA_ROPE
    c_kr = q_rank + kv_rank
    w_kr = w_in_l[:, c_kr:c_kr + MLA_ROPE]
    slot = lambda w: jnp.concatenate([jnp.zeros((d, MLA_NOPE), w.dtype), w, jnp.zeros((d, pad_hi), w.dtype)], axis=1)
    w_mla = jnp.concatenate([w_in_l[:, 0:c_kr], slot(w_kr), slot(_rotate_half_cols(w_kr)),
                             w_in_l[:, c_kr + MLA_ROPE:c_kr + MLA_ROPE + width]], axis=1)
    uq = w_uq_l.reshape(q_rank, MLA_HEADS, MLA_NOPE + MLA_ROPE)
    uq_n, uq_r = uq[..., :MLA_NOPE], uq[..., MLA_NOPE:]
    zn = jnp.zeros_like(uq_n)
    zp = jnp.zeros((q_rank, MLA_HEADS, pad_hi), uq.dtype)
    main = jnp.concatenate([uq_n, uq_r, zp], axis=-1).reshape(q_rank, MLA_HEADS * MLA_SLOT)
    part = jnp.concatenate([zn, _rotate_half_cols(uq_r), zp], axis=-1).reshape(q_rank, MLA_HEADS * MLA_SLOT)
    wuq2 = jnp.concatenate([main, part], axis=1)
    ukv = w_ukv_l.reshape(kv_rank, MLA_HEADS, MLA_NOPE + MLA_V)
    kn = ukv[..., :MLA_NOPE]
    wukn = jnp.concatenate([kn, jnp.zeros((kv_rank, MLA_HEADS, MLA_SLOT - MLA_NOPE), kn.dtype)], axis=-1)
    wukn = wukn.reshape(kv_rank, MLA_HEADS * MLA_SLOT)
    wvt = ukv[..., MLA_NOPE:].reshape(kv_rank, width).T
    return w_mla, wuq2, wukn, wvt


def kernel(x, c, ctx, c_ctx, ada_w, ada_b, norm_g, w_in, mla_q_norm_g, mla_kv_norm_g, mla_w_uq, mla_w_ukv,
           hg_lb_logits, hg_norm_g, conv_w, conv_b, w_branch, w_out, final_norm_g):
    bsz, n_lat, d = x.shape
    n_ctx = ctx.shape[1]
    n = n_lat + n_ctx
    depth = w_in.shape[0]
    q_rank = mla_q_norm_g.shape[-1]
    kv_rank = mla_kv_norm_g.shape[-1]
    width = MLA_HEADS * MLA_V
    kw = HG_HEADS * HG_DK
    vw = HG_HEADS * HG_DV
    cw = conv_w.shape[-1]
    assert n_lat % ROW_TILE == 0 and n_ctx % ROW_TILE == 0 and n_lat % GRID_W == 0
    assert width == vw == cw == w_branch.shape[2]

    pad = (-(bsz + 1)) % 8
    cs = jnp.concatenate([c, c_ctx[None, :], jnp.zeros((pad, d), c.dtype)], axis=0).astype(F32)
    mod = _modulation(cs, ada_w.astype(F32), ada_b.astype(F32))
    mod3 = [mod[l].reshape(mod.shape[1], 1, 3 * d) for l in range(depth)]
    lb_all = _lower_bounds(hg_lb_logits)
    cos_t, sin_t = _rope_tables(n_lat, n_ctx)

    c_hg = q_rank + kv_rank + MLA_ROPE + width
    c_cv = c_hg + 2 * kw + vw + kw + vw
    c_gate = c_cv + 4 * cw
    w_in_b = w_in.astype(BF16)
    w_branch_b = w_branch.astype(BF16)
    w_out_b = w_out.astype(BF16)
    norm_gs = [norm_g[l].reshape(1, d).astype(F32) for l in range(depth)]
    final_g = final_norm_g.reshape(1, d).astype(F32)

    h = jnp.concatenate([x, ctx], axis=1).astype(F32)
    tm_merge = 768 if n % 768 == 0 else ROW_TILE
    tm_last = 1024 if n_lat % 1024 == 0 else ROW_TILE
    u = _norm_mod(h, mod3[0], norm_gs[0], n_lat, ROW_TILE)
    for l in range(depth):
        last = l == depth - 1
        n_out = n_lat if last else n
        w_mla, wuq2, wukn, wvt = _mla_weights(w_in_b[l], mla_w_uq[l].astype(BF16), mla_w_ukv[l].astype(BF16),
                                              q_rank, kv_rank)
        y_mla = _mla(u, w_mla, mla_q_norm_g[l].reshape(1, q_rank).astype(F32),
                     mla_kv_norm_g[l].reshape(1, kv_rank).astype(F32), wuq2, wukn, wvt, cos_t, sin_t, n_lat, n_out)
        y_hg = _hgrn(u, w_in_b[l, :, c_hg:c_cv], lb_all[l].reshape(1, 2 * kw),
                     hg_norm_g[l].reshape(1, vw).astype(F32), n_lat, n_out)
        y_cv = _conv(u, w_in_b[l, :, c_cv:c_gate], conv_w[l].astype(F32), conv_b[l].reshape(1, cw).astype(F32),
                     n_lat, n_out)
        if last:
            h, = _merge(u, y_mla, y_hg, y_cv, h, mod3[l], w_in_b[l, :, c_gate:], w_branch_b[l], w_out_b[l],
                        mod3[l], final_g, n_lat, tm_last, True)
        else:
            h, u = _merge(u, y_mla, y_hg, y_cv, h, mod3[l], w_in_b[l, :, c_gate:], w_branch_b[l], w_out_b[l],
                          mod3[l + 1], norm_gs[l + 1], n_lat, tm_merge, False)
    return h
```

```python
import functools

import numpy as np
import jax
import jax.numpy as jnp
from jax import lax
from jax.experimental import pallas as pl
from jax.experimental.pallas import tpu as pltpu

F32 = jnp.float32
BF16 = jnp.bfloat16

RMS_EPS = 1e-6
GRID_W = 64
ROPE_BASE = 10000.0
MLA_HEADS = 8
MLA_NOPE = 64
MLA_ROPE = 32
MLA_V = 64
MLA_SLOT = 128
MLA_KEY_BLOCK = 768
MLA_SCALE = (MLA_NOPE + MLA_ROPE) ** -0.5
LOG2_E = float(np.log2(np.e))
MLA_SCALE_LOG2 = MLA_SCALE * LOG2_E
HG_HEADS = 8
HG_DK = 64
HG_DV = 64
HG_GROUP = 256
HG_CHUNK = 32
CV_K = 3

V7X_VMEM_BYTES = 64 * 1024 * 1024
V7X_LANES = 128
ROW_TILE = 256
MERGE_SUB = 256


def _vmem_limit(nbytes):
    return int(min(nbytes + (12 << 20), V7X_VMEM_BYTES - (6 << 20)))


def _dot(a, b):
    return jnp.dot(a, b, preferred_element_type=F32)


def _dot_nt(a, b):
    return lax.dot_general(a, b, (((1,), (1,)), ((), ())), preferred_element_type=F32)


def _sigmoid(x):
    return 1.0 / (1.0 + jnp.exp(-x))


def _silu(x):
    return x * _sigmoid(x)


def _split2(x):
    hi = x.astype(BF16)
    lo = (x - hi.astype(F32)).astype(BF16)
    return hi, lo


def _mod_norm(x, g, mod_b, mod_c, is_lat):
    d = x.shape[-1]
    y = x * lax.rsqrt(jnp.mean(x * x, axis=-1, keepdims=True) + RMS_EPS) * g
    shift = jnp.where(is_lat, mod_b[:, 0:d], mod_c[:, 0:d])
    scale = jnp.where(is_lat, mod_b[:, d:2 * d], mod_c[:, d:2 * d])
    return (y * (1.0 + scale) + shift).astype(BF16)


def _mod_kernel(c_ref, w_ref, b_ref, o_ref):
    s = _silu(c_ref[...])
    o_ref[0] = jnp.dot(s, w_ref[0], preferred_element_type=F32, precision=lax.Precision.HIGHEST) + b_ref[0]


def _modulation(cs, ada_w, ada_b):
    depth, d, d3 = ada_w.shape
    rows = cs.shape[0]
    nb = d3 // d
    return pl.pallas_call(
        _mod_kernel,
        out_shape=jax.ShapeDtypeStruct((depth, rows, d3), F32),
        grid=(depth, nb),
        in_specs=[
            pl.BlockSpec((rows, d), lambda l, j: (0, 0)),
            pl.BlockSpec((1, d, d), lambda l, j: (l, 0, j)),
            pl.BlockSpec((1, 1, d), lambda l, j: (l, 0, j)),
        ],
        out_specs=pl.BlockSpec((1, rows, d), lambda l, j: (l, 0, j)),
        compiler_params=pltpu.CompilerParams(
            dimension_semantics=("arbitrary", "arbitrary"),
            vmem_limit_bytes=_vmem_limit(2 * d * d * 4 + 4 * rows * d * 4),
        ),
        name="adaln_mod",
    )(cs, ada_w, ada_b.reshape(depth, 1, d3))


def _lb_kernel(x_ref, o_ref):
    x = x_ref[...]
    m = jnp.max(x, axis=0, keepdims=True)
    e = jnp.exp(x - m)
    p = e / jnp.sum(e, axis=0, keepdims=True)
    depth = x.shape[0]
    rows = [p[0:1]]
    for l in range(1, depth):
        rows.append(rows[-1] + p[l:l + 1])
    cs = jnp.concatenate(rows, axis=0)
    o_ref[...] = cs - cs[0:1]


def _lower_bounds(hg_lb_logits):
    depth = hg_lb_logits.shape[0]
    flat = hg_lb_logits.reshape(depth, -1).astype(F32)
    return pl.pallas_call(
        _lb_kernel,
        out_shape=jax.ShapeDtypeStruct(flat.shape, F32),
        name="hgrn_lower_bounds",
    )(flat)


def _entry_kernel(x_ref, c_ref, modb_ref, modc_ref, g_ref, h_ref, u_ref, *, lat_tiles):
    is_lat = pl.program_id(1) < lat_tiles
    rows = jnp.where(is_lat, x_ref[0], c_ref[0]).astype(F32)
    h_ref[0] = rows
    u_ref[0] = _mod_norm(rows, g_ref[...], modb_ref[0], modc_ref[0], is_lat)


def _entry(x, ctx, mod3, norm_g, tm):
    bsz, n_lat, d = x.shape
    n = n_lat + ctx.shape[1]
    lat_tiles = n_lat // tm
    ctx_row = bsz
    tok = pl.BlockSpec((1, tm, d), lambda b, j: (b, j, 0))
    return pl.pallas_call(
        functools.partial(_entry_kernel, lat_tiles=lat_tiles),
        out_shape=[jax.ShapeDtypeStruct((bsz, n, d), F32), jax.ShapeDtypeStruct((bsz, n, d), BF16)],
        grid=(bsz, n // tm),
        in_specs=[
            pl.BlockSpec((1, tm, d), lambda b, j: (b, jnp.minimum(j, lat_tiles - 1), 0)),
            pl.BlockSpec((1, tm, d), lambda b, j: (b, jnp.maximum(j - lat_tiles, 0), 0)),
            pl.BlockSpec((1, 1, 3 * d), lambda b, j: (b, 0, 0)),
            pl.BlockSpec((1, 1, 3 * d), lambda b, j: (ctx_row, 0, 0)),
            pl.BlockSpec((1, d), lambda b, j: (0, 0)),
        ],
        out_specs=[tok, tok],
        compiler_params=pltpu.CompilerParams(
            dimension_semantics=("arbitrary", "arbitrary"),
            vmem_limit_bytes=_vmem_limit(2 * tm * d * (4 + 4 + 4 + 2) + 4 * tm * d * 4),
        ),
        name="entry_norm",
    )(x, ctx, mod3, mod3, norm_g)


def _mla_kernel(u_ref, w_ref, gq_ref, gkv_ref, wuq_ref, wukn_ref, wvt_ref, cos_ref, sin_ref, out_ref,
                q_s, k_s, vt_s, g_s, ot_s, st_s, *, n_lat, n_out, q_rank, kv_rank):
    n = u_ref.shape[1]
    n_ctx = n - n_lat
    hs = MLA_HEADS * MLA_SLOT
    width = MLA_HEADS * MLA_V
    c_kv = q_rank
    c_kr = q_rank + kv_rank
    c_krs = c_kr + MLA_SLOT
    c_g = c_krs + MLA_SLOT

    for t in range(n // ROW_TILE):
        rows = slice(t * ROW_TILE, (t + 1) * ROW_TILE)
        z = _dot(u_ref[0, rows, :], w_ref[...])
        cos_t = cos_ref[rows, :]
        sin_t = sin_ref[rows, :]
        cos8 = jnp.concatenate([cos_t] * MLA_HEADS, axis=1)
        sin8 = jnp.concatenate([sin_t] * MLA_HEADS, axis=1)
        ckv = z[:, c_kv:c_kr]
        kvn = (ckv * lax.rsqrt(jnp.mean(ckv * ckv, axis=-1, keepdims=True) + RMS_EPS) * gkv_ref[...]).astype(BF16)
        kn = _dot(kvn, wukn_ref[...])
        kr = z[:, c_kr:c_krs] * cos_t + z[:, c_krs:c_g] * sin_t
        k = (kn + jnp.concatenate([kr] * MLA_HEADS, axis=1)).astype(BF16)
        for h in range(MLA_HEADS):
            k_s[h, rows, :] = k[:, h * MLA_SLOT:(h + 1) * MLA_SLOT]
        vt = _dot_nt(wvt_ref[...], kvn).astype(BF16)
        for h in range(MLA_HEADS):
            vt_s[h, :, rows] = vt[h * MLA_V:(h + 1) * MLA_V, :]
        if t * ROW_TILE < n_out:
            cq = z[:, 0:q_rank]
            cqn = (cq * lax.rsqrt(jnp.mean(cq * cq, axis=-1, keepdims=True) + RMS_EPS) * gq_ref[...]).astype(BF16)
            q2 = _dot(cqn, wuq_ref[...])
            q = ((q2[:, 0:hs] * cos8 + q2[:, hs:2 * hs] * sin8) * MLA_SCALE_LOG2).astype(BF16)
            for h in range(MLA_HEADS):
                q_s[h, rows, :] = q[:, h * MLA_SLOT:(h + 1) * MLA_SLOT]
            g_s[rows, :] = _silu(z[:, c_g:c_g + width])

    def key_block(nk):
        return MLA_KEY_BLOCK if nk % MLA_KEY_BLOCK == 0 else nk

    def scores_block(h, r0, k0, kb, kbs, m):
        st = _dot_nt(k_s[h, k0 + kb:k0 + kb + kbs, :], q_s[h, pl.ds(r0, ROW_TILE), :])
        st_s[h % 2, kb:kb + kbs, :] = st
        bm = jnp.max(st, axis=0, keepdims=True)
        return bm if m is None else jnp.maximum(m, bm)

    def scores(h, r0, k0, nk):
        m = None
        kbs = key_block(nk)
        for kb in range(0, nk, kbs):
            m = scores_block(h, r0, k0, kb, kbs, m)
        return m

    def attend(r0, k0, nk, m0, r0_next):
        rows = pl.ds(r0, ROW_TILE)
        kbs = key_block(nk)
        m_next = m0
        for h in range(MLA_HEADS):
            m = m_next
            m_next = None
            l = None
            ot = None
            for kb in range(0, nk, kbs):
                if h + 1 < MLA_HEADS:
                    m_next = scores_block(h + 1, r0, k0, kb, kbs, m_next)
                elif r0_next is not None:
                    m_next = scores_block(0, r0_next, k0, kb, kbs, m_next)
                p = jnp.exp2(st_s[h % 2, kb:kb + kbs, :] - m)
                bl = jnp.sum(p, axis=0, keepdims=True)
                l = bl if l is None else l + bl
                bo = _dot(vt_s[h, :, k0 + kb:k0 + kb + kbs], p.astype(BF16))
                ot = bo if ot is None else ot + bo
            ot_s[h] = ot * (1.0 / l)
        o = jnp.concatenate([ot_s[h] for h in range(MLA_HEADS)], axis=0).T
        out_ref[0, rows, :] = (o * g_s[rows, :]).astype(BF16)
        return m_next

    lat_tiles = n_lat // ROW_TILE

    def lat_tile(t, m0):
        r0 = pl.multiple_of(t * ROW_TILE, ROW_TILE)
        r0_next = pl.multiple_of(jnp.minimum(t + 1, lat_tiles - 1) * ROW_TILE, ROW_TILE)
        return attend(r0, 0, n, m0, r0_next)

    lax.fori_loop(0, lat_tiles, lat_tile, scores(0, 0, 0, n))
    for t in range((n_out - n_lat) // ROW_TILE):
        r0 = n_lat + t * ROW_TILE
        attend(r0, n_lat, n_ctx, scores(0, r0, n_lat, n_ctx), None)


def _mla(u, w_mla, gq, gkv, wuq2, wukn, wvt, cos_t, sin_t, n_lat, n_out):
    bsz, n, d = u.shape
    q_rank, kv_rank = gq.shape[-1], gkv.shape[-1]
    width = MLA_HEADS * MLA_V
    hs = MLA_HEADS * MLA_SLOT
    est = (2 * n * d * 2 + 2 * w_mla.size * 2 + 2 * (wuq2.size + wukn.size + wvt.size) * 2
           + 4 * n * MLA_SLOT * 4 + 2 * n_out * width * 2
           + 2 * n * hs * 2 + n * width * 2 + n * width * 4 + width * ROW_TILE * 4
           + 2 * n * ROW_TILE * 4 + 2 * n * ROW_TILE * 4)
    full = lambda a: pl.BlockSpec(a.shape, lambda b: (0,) * a.ndim)
    return pl.pallas_call(
        functools.partial(_mla_kernel, n_lat=n_lat, n_out=n_out, q_rank=q_rank, kv_rank=kv_rank),
        out_shape=jax.ShapeDtypeStruct((bsz, n_out, width), BF16),
        grid=(bsz,),
        in_specs=[
            pl.BlockSpec((1, n, d), lambda b: (b, 0, 0)),
            full(w_mla), full(gq), full(gkv), full(wuq2), full(wukn), full(wvt), full(cos_t), full(sin_t),
        ],
        out_specs=pl.BlockSpec((1, n_out, width), lambda b: (b, 0, 0)),
        scratch_shapes=[
            pltpu.VMEM((MLA_HEADS, n, MLA_SLOT), BF16),
            pltpu.VMEM((MLA_HEADS, n, MLA_SLOT), BF16),
            pltpu.VMEM((MLA_HEADS, MLA_V, n), BF16),
            pltpu.VMEM((n, width), F32),
            pltpu.VMEM((MLA_HEADS, MLA_V, ROW_TILE), F32),
            pltpu.VMEM((2, n, ROW_TILE), F32),
        ],
        compiler_params=pltpu.CompilerParams(
            dimension_semantics=("arbitrary",), vmem_limit_bytes=_vmem_limit(est)),
        name="mla_mixer",
    )(u, w_mla, gq, gkv, wuq2, wukn, wvt, cos_t, sin_t)


def _hgrn_kernel(u_ref, w_ref, lb_ref, gn_ref, out_ref,
                 o_s, st_s, q_s, v_s, gate_s, k_s, b_s, dt_s, zf_s, tot_s, *, n_lat, n_out):
    n = u_ref.shape[1]
    kw = HG_HEADS * HG_DK
    vw = HG_HEADS * HG_DV
    c_f = kw + vw
    c_gate = c_f + 2 * kw
    n_tiles = n // ROW_TILE
    lat_tiles = n_lat // ROW_TILE
    ctx_tiles = n_tiles - lat_tiles
    n_chunk = ROW_TILE // HG_CHUNK
    n_grp = kw // HG_GROUP
    heads_per_grp = HG_GROUP // HG_DK
    stack = heads_per_grp * HG_CHUNK

    ti = lax.broadcasted_iota(jnp.int32, (HG_CHUNK + 16, HG_CHUNK), 0)
    si = lax.broadcasted_iota(jnp.int32, (HG_CHUNK + 16, HG_CHUNK), 1)
    tri = tuple(jnp.where(((ti < HG_CHUNK) & m) | (ti == HG_CHUNK), 1.0, 0.0).astype(BF16)
                for m in (si <= ti, si >= ti))
    hr = lax.broadcasted_iota(jnp.int32, (stack, HG_GROUP), 0) // HG_CHUNK
    hc = lax.broadcasted_iota(jnp.int32, (stack, HG_GROUP), 1) // HG_DK
    head_mask = jnp.where(hr == hc, 1.0, 0.0).astype(BF16)
    hl = lax.broadcasted_iota(jnp.int32, (1, HG_GROUP // 2), 1) // HG_DK
    half_mask = (jnp.where(hl == 0, 1.0, 0.0), jnp.where(hl == 1, 1.0, 0.0))
    at = lax.broadcasted_iota(jnp.int32, (HG_CHUNK, stack), 0)
    as_ = lax.broadcasted_iota(jnp.int32, (HG_CHUNK, stack), 1) % HG_CHUNK
    causal = (as_ <= at, as_ >= at)
    hq = HG_GROUP // 2
    zq = jnp.zeros((hq, hq), BF16)
    ref_row = (HG_CHUNK // 2 - 1, HG_CHUNK // 2)
    end_row = (HG_CHUNK - 1, 0)

    lb_all = lb_ref[...]

    qvg_tile = 768 if n % 768 == 0 else ROW_TILE

    def qvg(t, carry):
        rows = pl.ds(pl.multiple_of(t * qvg_tile, qvg_tile), qvg_tile)
        ut = u_ref[0, rows, :]
        qv = _dot(ut, w_ref[:, 0:c_f])
        q_s[rows, :] = qv[:, 0:kw]
        v_s[rows, :] = qv[:, kw:c_f].astype(BF16)
        gate_s[rows, :] = _silu(_dot(ut, w_ref[:, c_gate:c_gate + vw])).astype(BF16)
        return carry

    lax.fori_loop(0, n // qvg_tile, qvg, 0)

    def tiles_of(step):
        t_fwd = jnp.where(step < ctx_tiles, lat_tiles + step, step - ctx_tiles)
        t_bwd = n_tiles - 1 - step
        return (t_fwd, t_bwd)

    def precompute(step, slot):
        tl = tiles_of(step)

        def logits():
            for d in range(2):
                rows = pl.ds(pl.multiple_of(tl[d] * ROW_TILE, ROW_TILE), ROW_TILE)
                zf_s[d] = _dot(u_ref[0, rows, :], w_ref[:, c_f + d * kw:c_f + (d + 1) * kw])

        def gates(cl):
            r = slice(cl * HG_CHUNK, (cl + 1) * HG_CHUNK)
            for d in range(2):
                idx = slot * 2 + d
                zf = zf_s[d, r, :]
                lb = lb_all[:, d * kw:(d + 1) * kw]
                e = jnp.exp(-jnp.abs(zf))
                log_sig = jnp.minimum(zf, 0.0) - jnp.log(1.0 + e)
                a = jnp.log(lb)
                c = jnp.log(1.0 - lb) + log_sig
                log_f = jnp.maximum(a, c) + jnp.log(1.0 + jnp.exp(-jnp.abs(a - c)))
                rcp = 1.0 / (1.0 + e)
                k_s[idx, r, :] = (1.0 - lb) * jnp.where(zf >= 0.0, e * rcp, rcp)
                g_hi, g_lo = _split2(log_f)
                cum = (_dot(tri[d], g_hi) + _dot(tri[d], g_lo)) * LOG2_E
                b_s[idx, r, :] = cum[0:HG_CHUNK]
                tot_s[d, cl:cl + 1, :] = cum[HG_CHUNK:HG_CHUNK + 1]

        def decays():
            for d in range(2):
                dt_s[slot * 2 + d] = jnp.exp2(tot_s[d]).T

        return [logits] + [functools.partial(gates, cl) for cl in range(n_chunk)] + [decays]

    def chunks(step, slot, between=()):
        tl = tiles_of(step)
        between = list(between)
        for cidx in range(n_chunk):
            for d in range(2):
                idx = slot * 2 + d
                cl = cidx if d == 0 else n_chunk - 1 - cidx
                c0 = cl * HG_CHUNK
                rows = slice(c0, c0 + HG_CHUNK)
                grow = pl.ds(pl.multiple_of(tl[d] * ROW_TILE + c0, HG_CHUNK), HG_CHUNK)
                for g in range(n_grp):
                    lanes = slice(g * HG_GROUP, (g + 1) * HG_GROUP)
                    q = q_s[grow, lanes]
                    v = v_s[grow, lanes]
                    k = k_s[idx, rows, lanes]
                    b = b_s[idx, rows, lanes]
                    b_ref = b_s[idx, c0 + ref_row[d]:c0 + ref_row[d] + 1, lanes]
                    b_end = b_s[idx, c0 + end_row[d]:c0 + end_row[d] + 1, lanes]
                    qd = (q * jnp.exp2(b - b_ref)).astype(BF16)
                    qs = (q * jnp.exp2(b)).astype(BF16)
                    kd = (k * jnp.exp2(b_ref - b)).astype(BF16)
                    kr = k * jnp.exp2(b_end - b)
                    kd_st = jnp.concatenate([kd] * heads_per_grp, axis=0) * head_mask
                    v_st = jnp.concatenate([v] * heads_per_grp, axis=0) * head_mask
                    kr_fold = jnp.concatenate(
                        [kr[:, (j // 2) * hq:(j // 2 + 1) * hq] * half_mask[j % 2] for j in range(heads_per_grp)],
                        axis=0)
                    kr_t = kr_fold.T.astype(BF16)
                    amat = jnp.where(causal[d], _dot_nt(qd, kd_st), 0.0).astype(BF16)
                    res = _dot(jnp.concatenate([kr_t, amat], axis=0), v_st)
                    sidx = d * n_grp + g
                    dtc = dt_s[idx, lanes, cl:cl + 1]
                    s0 = st_s[sidx, 0]
                    s1 = st_s[sidx, 1]
                    s_bf = jnp.concatenate(
                        [jnp.concatenate([s0.astype(BF16), zq], axis=1),
                         jnp.concatenate([zq, s1.astype(BF16)], axis=1)], axis=0)
                    o_s[d, grow, lanes] = res[hq:] + _dot(qs, s_bf)
                    st_s[sidx, 0] = s0 * dtc[0:hq] + res[0:hq, 0:hq]
                    st_s[sidx, 1] = s1 * dtc[hq:HG_GROUP] + res[0:hq, hq:HG_GROUP]
            if between:
                between.pop(0)()
        for piece in between:
            piece()

    def step(i, slot, has_next):
        if has_next:
            pieces = precompute(i + 1, 1 - slot)
            pieces[0]()
            chunks(i, slot, pieces[1:])
        else:
            chunks(i, slot)

    st_s[...] = jnp.zeros_like(st_s)
    tot_s[...] = jnp.zeros_like(tot_s)
    for piece in precompute(0, 0):
        piece()

    def step_pair(j, carry):
        step(2 * j, 0, True)
        step(2 * j + 1, 1, True)
        return carry

    n_pairs = (n_tiles - 1) // 2
    lax.fori_loop(0, n_pairs, step_pair, 0)
    if 2 * n_pairs == n_tiles - 1:
        step(n_tiles - 1, 0, False)
    else:
        step(n_tiles - 2, 0, True)
        step(n_tiles - 1, 1, False)

    hi_ = lax.broadcasted_iota(jnp.int32, (vw, vw), 0) // HG_DV
    hj_ = lax.broadcasted_iota(jnp.int32, (vw, vw), 1) // HG_DV
    head_sum = jnp.where(hi_ == hj_, 1.0, 0.0).astype(BF16)

    def finish(t, carry):
        rows = pl.ds(pl.multiple_of(t * ROW_TILE, ROW_TILE), ROW_TILE)
        o = o_s[0, rows, :] + o_s[1, rows, :]
        sq_hi, sq_lo = _split2(o * o)
        ms = (_dot(sq_hi, head_sum) + _dot(sq_lo, head_sum)) * (1.0 / HG_DV)
        y = o * lax.rsqrt(ms + RMS_EPS) * gn_ref[...]
        out_ref[0, rows, :] = (y * gate_s[rows, :].astype(F32)).astype(BF16)
        return carry

    lax.fori_loop(0, n_out // ROW_TILE, finish, 0)


def _hgrn(u, w_hg, lb, gn, n_lat, n_out):
    bsz, n, d = u.shape
    kw = HG_HEADS * HG_DK
    vw = HG_HEADS * HG_DV
    n_streams = 2 * (kw // HG_GROUP)
    est = (2 * n * d * 2 + 2 * w_hg.size * 2 + 2 * n_out * vw * 2
           + 2 * n * vw * 4 + n_streams * HG_GROUP * HG_GROUP * 2 + n * kw * 4 + n * 2 * vw * 2
           + 8 * ROW_TILE * kw * 4 + 4 * kw * V7X_LANES * 4 + 6 * ROW_TILE * kw * 4)
    full = lambda a: pl.BlockSpec(a.shape, lambda b: (0,) * a.ndim)
    return pl.pallas_call(
        functools.partial(_hgrn_kernel, n_lat=n_lat, n_out=n_out),
        out_shape=jax.ShapeDtypeStruct((bsz, n_out, vw), BF16),
        grid=(bsz,),
        in_specs=[pl.BlockSpec((1, n, d), lambda b: (b, 0, 0)), full(w_hg), full(lb), full(gn)],
        out_specs=pl.BlockSpec((1, n_out, vw), lambda b: (b, 0, 0)),
        scratch_shapes=[
            pltpu.VMEM((2, n, vw), F32),
            pltpu.VMEM((n_streams, 2, HG_GROUP // 2, HG_GROUP // 2), F32),
            pltpu.VMEM((n, kw), F32),
            pltpu.VMEM((n, vw), BF16),
            pltpu.VMEM((n, vw), BF16),
            pltpu.VMEM((4, ROW_TILE, kw), F32),
            pltpu.VMEM((4, ROW_TILE, kw), F32),
            pltpu.VMEM((4, kw, V7X_LANES), F32),
            pltpu.VMEM((2, ROW_TILE, kw), F32),
            pltpu.VMEM((2, V7X_LANES, kw), F32),
        ],
        compiler_params=pltpu.CompilerParams(
            dimension_semantics=("arbitrary",), vmem_limit_bytes=_vmem_limit(est)),
        name="hgrn_mixer",
    )(u, w_hg, lb, gn)


CV_PAD = 8


def _conv_kernel(u_ref, w_ref, cw_ref, cb_ref, out_ref, uu_s, coef_s, *, n_lat, n_out, tile):
    n = u_ref.shape[1]
    cw = cw_ref.shape[-1]
    out_tiles = n_out // tile
    zero = jnp.zeros((CV_PAD, cw), F32)
    uu_s[0:CV_PAD, :] = zero
    uu_s[CV_PAD + n_lat:2 * CV_PAD + n_lat, :] = zero
    uu_s[2 * CV_PAD + n:3 * CV_PAD + n, :] = zero

    def srow(r):
        return r + CV_PAD + (CV_PAD if r >= n_lat else 0)

    def pieces(t):
        r0, r1 = t * tile, (t + 1) * tile
        cand = ((r0, min(r1, n_lat)), (max(r0, n_lat), r1))
        return [(a, b) for a, b in cand if a < b]

    w0 = cw_ref[0:1, :]
    w1 = cw_ref[1:2, :]
    w2 = cw_ref[2:3, :]
    bias = cb_ref[...]

    def project(t):
        r0 = t * tile
        z = _dot(u_ref[0, r0:r0 + tile, :], w_ref[...])
        uu = z[:, 2 * cw:3 * cw] * z[:, 0:cw]
        for a, b in pieces(t):
            uu_s[srow(a):srow(a) + (b - a), :] = uu[a - r0:b - r0]
        coef_s[r0:r0 + tile, :] = z[:, cw:2 * cw] * _silu(z[:, 3 * cw:4 * cw])

    def emit(t):
        for a, b in pieces(t):
            o, m = srow(a), b - a
            conv = uu_s[o - 1:o - 1 + m, :] * w0 + uu_s[o:o + m, :] * w1 + uu_s[o + 1:o + 1 + m, :] * w2 + bias
            out_ref[0, a:b, :] = (coef_s[a:b, :] * conv).astype(BF16)

    for t in range(out_tiles):
        project(t)
        if t >= 1:
            emit(t - 1)
    emit(out_tiles - 1)


def _conv(u, w_cv, conv_w, conv_b, n_lat, n_out):
    bsz, n, d = u.shape
    cw = conv_w.shape[-1]
    tile = next(t for t in (768, 1024, ROW_TILE) if n_out % t == 0)
    est = (2 * n * d * 2 + 2 * w_cv.size * 2 + 2 * n_out * cw * 2 + (n + 3 * CV_PAD) * cw * 4 + n * cw * 4
           + 2 * tile * 4 * cw * 4)
    full = lambda a: pl.BlockSpec(a.shape, lambda b: (0,) * a.ndim)
    return pl.pallas_call(
        functools.partial(_conv_kernel, n_lat=n_lat, n_out=n_out, tile=tile),
        out_shape=jax.ShapeDtypeStruct((bsz, n_out, cw), BF16),
        grid=(bsz,),
        in_specs=[pl.BlockSpec((1, n, d), lambda b: (b, 0, 0)), full(w_cv), full(conv_w), full(conv_b)],
        out_specs=pl.BlockSpec((1, n_out, cw), lambda b: (b, 0, 0)),
        scratch_shapes=[pltpu.VMEM((n + 3 * CV_PAD, cw), F32), pltpu.VMEM((n, cw), F32)],
        compiler_params=pltpu.CompilerParams(
            dimension_semantics=("arbitrary",), vmem_limit_bytes=_vmem_limit(est)),
        name="conv_mixer",
    )(u, w_cv, conv_w, conv_b)


def _merge_kernel(u_ref, ym_ref, yh_ref, yc_ref, h_ref, modb_ref, modc_ref, wg_ref, wb_ref, wo_ref,
                  nmodb_ref, nmodc_ref, ng_ref, *out_refs, n_lat, tm, last):
    d = h_ref.shape[-1]
    sub = MERGE_SUB if tm % MERGE_SUB == 0 else tm

    def project(s):
        rows = slice(s * sub, (s + 1) * sub)
        u = u_ref[0, rows, :]
        acc = None
        for i, y_ref in enumerate((ym_ref, yh_ref, yc_ref)):
            gate_i = _sigmoid(_dot(u, wg_ref[:, i * d:(i + 1) * d]))
            term = gate_i * _dot(y_ref[0, rows, :], wb_ref[i])
            acc = term if acc is None else acc + term
        return _dot(acc.astype(BF16), wo_ref[...])

    def finish(s, out):
        rows = slice(s * sub, (s + 1) * sub)
        row = pl.program_id(1) * tm + s * sub + lax.broadcasted_iota(jnp.int32, (sub, 1), 0)
        is_lat = row < n_lat
        gate = jnp.where(is_lat, modb_ref[0][:, 2 * d:3 * d], modc_ref[0][:, 2 * d:3 * d])
        hn = h_ref[0, rows, :] + gate * out
        if last:
            o_ref, = out_refs
            o_ref[0, rows, :] = hn * lax.rsqrt(jnp.mean(hn * hn, axis=-1, keepdims=True) + RMS_EPS) * ng_ref[...]
        else:
            o_ref, un_ref = out_refs
            o_ref[0, rows, :] = hn
            un_ref[0, rows, :] = _mod_norm(hn, ng_ref[...], nmodb_ref[0], nmodc_ref[0], is_lat)

    pending = None
    for s in range(tm // sub):
        out = project(s)
        if pending is not None:
            finish(*pending)
        pending = (s, out)
    finish(*pending)


def _merge(u, ym, yh, yc, h, mod3, w_gate, w_branch, w_out, next_mod3, next_g, n_lat, tm, last):
    bsz, n, d = h.shape
    bw = ym.shape[-1]
    n_out = n_lat if last else n
    ctx_row = bsz
    tok = lambda w: pl.BlockSpec((1, tm, w), lambda b, j: (b, j, 0))
    modb = pl.BlockSpec((1, 1, 3 * d), lambda b, j: (b, 0, 0))
    modc = pl.BlockSpec((1, 1, 3 * d), lambda b, j: (ctx_row, 0, 0))
    full = lambda a: pl.BlockSpec(a.shape, lambda b, j: (0,) * a.ndim)
    once = lambda a: pl.BlockSpec(a.shape, lambda b, j: (0,) * a.ndim, pipeline_mode=pl.Buffered(1))
    est = (2 * tm * (d * 2 + 3 * bw * 2 + 2 * d * 4 + d * 2) + (w_gate.size + w_branch.size + w_out.size) * 2
           + 6 * tm * d * 4)
    out_shape = [jax.ShapeDtypeStruct((bsz, n_out, d), F32)]
    out_specs = [tok(d)]
    if not last:
        out_shape.append(jax.ShapeDtypeStruct((bsz, n_out, d), BF16))
        out_specs.append(tok(d))
    return pl.pallas_call(
        functools.partial(_merge_kernel, n_lat=n_lat, tm=tm, last=last),
        out_shape=out_shape,
        grid=(bsz, n_out // tm),
        in_specs=[
            tok(d), tok(bw), tok(bw), tok(bw), tok(d), modb, modc,
            once(w_gate), once(w_branch), once(w_out), modb, modc, full(next_g),
        ],
        out_specs=out_specs,
        compiler_params=pltpu.CompilerParams(
            dimension_semantics=("arbitrary", "arbitrary"), vmem_limit_bytes=_vmem_limit(est)),
        name="merge_out",
    )(u, ym, yh, yc, h, mod3, mod3, w_gate, w_branch, w_out, next_mod3, next_mod3, next_g)


def _rope_tables(n_lat, n_ctx):
    pairs = MLA_ROPE // 4
    rows = n_lat // GRID_W
    row_id = np.repeat(np.arange(rows, dtype=np.float32), GRID_W)
    col_id = np.tile(np.arange(GRID_W, dtype=np.float32), rows)
    inv_freq = jnp.power(ROPE_BASE, -jnp.arange(pairs, dtype=F32) / pairs)
    ang = jnp.stack([row_id[:, None] * inv_freq, col_id[:, None] * inv_freq], axis=1)
    ang = jnp.broadcast_to(ang[:, :, None, :], (n_lat, 2, 2, pairs)).reshape(n_lat, MLA_ROPE)
    pad_hi = MLA_SLOT - MLA_NOPE - MLA_ROPE
    cos = jnp.concatenate([jnp.ones((n_lat, MLA_NOPE), F32), jnp.cos(ang), jnp.ones((n_lat, pad_hi), F32)], axis=1)
    sin = jnp.concatenate([jnp.zeros((n_lat, MLA_NOPE), F32), jnp.sin(ang), jnp.zeros((n_lat, pad_hi), F32)], axis=1)
    cos = jnp.concatenate([cos, jnp.ones((n_ctx, MLA_SLOT), F32)], axis=0)
    sin = jnp.concatenate([sin, jnp.zeros((n_ctx, MLA_SLOT), F32)], axis=0)
    return cos, sin


def _rotate_half_cols(w):
    pairs = MLA_ROPE // 4
    ws = w.reshape(w.shape[:-1] + (2, 2, pairs))
    return jnp.stack([-ws[..., 1, :], ws[..., 0, :]], axis=-2).reshape(w.shape)


def _mla_weights(w_in_l, w_uq_l, w_ukv_l, q_rank, kv_rank):
    d = w_in_l.shape[0]
    width = MLA_HEADS * MLA_V
    pad_hi = MLA_SLOT - MLA_NOPE - MLA_ROPE
    c_kr = q_rank + kv_rank
    w_kr = w_in_l[:, c_kr:c_kr + MLA_ROPE]
    slot = lambda w: jnp.concatenate([jnp.zeros((d, MLA_NOPE), w.dtype), w, jnp.zeros((d, pad_hi), w.dtype)], axis=1)
    w_mla = jnp.concatenate([w_in_l[:, 0:c_kr], slot(w_kr), slot(_rotate_half_cols(w_kr)),
                             w_in_l[:, c_kr + MLA_ROPE:c_kr + MLA_ROPE + width]], axis=1)
    uq = w_uq_l.reshape(q_rank, MLA_HEADS, MLA_NOPE + MLA_ROPE)
    uq_n, uq_r = uq[..., :MLA_NOPE], uq[..., MLA_NOPE:]
    zn = jnp.zeros_like(uq_n)
    zp = jnp.zeros((q_rank, MLA_HEADS, pad_hi), uq.dtype)
    main = jnp.concatenate([uq_n, uq_r, zp], axis=-1).reshape(q_rank, MLA_HEADS * MLA_SLOT)
    part = jnp.concatenate([zn, _rotate_half_cols(uq_r), zp], axis=-1).reshape(q_rank, MLA_HEADS * MLA_SLOT)
    wuq2 = jnp.concatenate([main, part], axis=1)
    ukv = w_ukv_l.reshape(kv_rank, MLA_HEADS, MLA_NOPE + MLA_V)
    kn = ukv[..., :MLA_NOPE]
    wukn = jnp.concatenate([kn, jnp.zeros((kv_rank, MLA_HEADS, MLA_SLOT - MLA_NOPE), kn.dtype)], axis=-1)
    wukn = wukn.reshape(kv_rank, MLA_HEADS * MLA_SLOT)
    wvt = ukv[..., MLA_NOPE:].reshape(kv_rank, width).T
    return w_mla, wuq2, wukn, wvt


def kernel(x, c, ctx, c_ctx, ada_w, ada_b, norm_g, w_in, mla_q_norm_g, mla_kv_norm_g, mla_w_uq, mla_w_ukv,
           hg_lb_logits, hg_norm_g, conv_w, conv_b, w_branch, w_out, final_norm_g):
    bsz, n_lat, d = x.shape
    n_ctx = ctx.shape[1]
    n = n_lat + n_ctx
    depth = w_in.shape[0]
    q_rank = mla_q_norm_g.shape[-1]
    kv_rank = mla_kv_norm_g.shape[-1]
    width = MLA_HEADS * MLA_V
    kw = HG_HEADS * HG_DK
    vw = HG_HEADS * HG_DV
    cw = conv_w.shape[-1]
    assert n_lat % ROW_TILE == 0 and n_ctx % ROW_TILE == 0 and n_lat % GRID_W == 0
    assert width == vw == cw == w_branch.shape[2]

    pad = (-(bsz + 1)) % 8
    cs = jnp.concatenate([c, c_ctx[None, :], jnp.zeros((pad, d), c.dtype)], axis=0).astype(F32)
    mod = _modulation(cs, ada_w.astype(F32), ada_b.astype(F32))
    mod3 = [mod[l].reshape(mod.shape[1], 1, 3 * d) for l in range(depth)]
    lb_all = _lower_bounds(hg_lb_logits)
    cos_t, sin_t = _rope_tables(n_lat, n_ctx)

    c_hg = q_rank + kv_rank + MLA_ROPE + width
    c_cv = c_hg + 2 * kw + vw + kw + vw
    c_gate = c_cv + 4 * cw
    w_in_b = w_in.astype(BF16)
    w_branch_b = w_branch.astype(BF16)
    w_out_b = w_out.astype(BF16)
    norm_gs = [norm_g[l].reshape(1, d).astype(F32) for l in range(depth)]
    final_g = final_norm_g.reshape(1, d).astype(F32)

    tm_merge = 768 if n % 768 == 0 else ROW_TILE
    tm_last = 1024 if n_lat % 1024 == 0 else ROW_TILE
    h, u = _entry(x, ctx, mod3[0], norm_gs[0], ROW_TILE)
    for l in range(depth):
        last = l == depth - 1
        n_out = n_lat if last else n
        w_mla, wuq2, wukn, wvt = _mla_weights(w_in_b[l], mla_w_uq[l].astype(BF16), mla_w_ukv[l].astype(BF16),
                                              q_rank, kv_rank)
        y_mla = _mla(u, w_mla, mla_q_norm_g[l].reshape(1, q_rank).astype(F32),
                     mla_kv_norm_g[l].reshape(1, kv_rank).astype(F32), wuq2, wukn, wvt, cos_t, sin_t, n_lat, n_out)
        y_hg = _hgrn(u, w_in_b[l, :, c_hg:c_cv], lb_all[l].reshape(1, 2 * kw),
                     hg_norm_g[l].reshape(1, vw).astype(F32), n_lat, n_out)
        y_cv = _conv(u, w_in_b[l, :, c_cv:c_gate], conv_w[l].astype(F32), conv_b[l].reshape(1, cw).astype(F32),
                     n_lat, n_out)
        if last:
            h, = _merge(u, y_mla, y_hg, y_cv, h, mod3[l], w_in_b[l, :, c_gate:], w_branch_b[l], w_out_b[l],
                        mod3[l], final_g, n_lat, tm_last, True)
        else:
            h, u = _merge(u, y_mla, y_hg, y_cv, h, mod3[l], w_in_b[l, :, c_gate:], w_branch_b[l], w_out_b[l],
                          mod3[l + 1], norm_gs[l + 1], n_lat, tm_merge, False)
    return h
```

```python
import functools

import numpy as np
import jax
import jax.numpy as jnp
from jax import lax
from jax.experimental import pallas as pl
from jax.experimental.pallas import tpu as pltpu

F32 = jnp.float32
BF16 = jnp.bfloat16

RMS_EPS = 1e-6
GRID_W = 64
ROPE_BASE = 10000.0
MLA_HEADS = 8
MLA_NOPE = 64
MLA_ROPE = 32
MLA_V = 64
MLA_SLOT = 128
MLA_KEY_BLOCK = 768
MLA_SCALE = (MLA_NOPE + MLA_ROPE) ** -0.5
LOG2_E = float(np.log2(np.e))
MLA_SCALE_LOG2 = MLA_SCALE * LOG2_E
HG_HEADS = 8
HG_DK = 64
HG_DV = 64
HG_GROUP = 256
HG_CHUNK = 32
CV_K = 3

V7X_VMEM_BYTES = 64 * 1024 * 1024
V7X_LANES = 128
ROW_TILE = 256
MERGE_SUB = 256


def _vmem_limit(nbytes):
    return int(min(nbytes + (12 << 20), V7X_VMEM_BYTES - (6 << 20)))


def _dot(a, b):
    return jnp.dot(a, b, preferred_element_type=F32)


def _dot_nt(a, b):
    return lax.dot_general(a, b, (((1,), (1,)), ((), ())), preferred_element_type=F32)


def _sigmoid(x):
    return 1.0 / (1.0 + jnp.exp(-x))


def _silu(x):
    return x * _sigmoid(x)


def _split2(x):
    hi = x.astype(BF16)
    lo = (x - hi.astype(F32)).astype(BF16)
    return hi, lo


def _mod_norm(x, g, mod_b, mod_c, is_lat):
    d = x.shape[-1]
    y = x * lax.rsqrt(jnp.mean(x * x, axis=-1, keepdims=True) + RMS_EPS) * g
    shift = jnp.where(is_lat, mod_b[:, 0:d], mod_c[:, 0:d])
    scale = jnp.where(is_lat, mod_b[:, d:2 * d], mod_c[:, d:2 * d])
    return (y * (1.0 + scale) + shift).astype(BF16)


def _mod_kernel(c_ref, w_ref, b_ref, o_ref):
    s = _silu(c_ref[...])
    o_ref[0] = jnp.dot(s, w_ref[0], preferred_element_type=F32, precision=lax.Precision.HIGHEST) + b_ref[0]


def _modulation(cs, ada_w, ada_b):
    depth, d, d3 = ada_w.shape
    rows = cs.shape[0]
    nb = d3 // d
    return pl.pallas_call(
        _mod_kernel,
        out_shape=jax.ShapeDtypeStruct((depth, rows, d3), F32),
        grid=(depth, nb),
        in_specs=[
            pl.BlockSpec((rows, d), lambda l, j: (0, 0)),
            pl.BlockSpec((1, d, d), lambda l, j: (l, 0, j)),
            pl.BlockSpec((1, 1, d), lambda l, j: (l, 0, j)),
        ],
        out_specs=pl.BlockSpec((1, rows, d), lambda l, j: (l, 0, j)),
        compiler_params=pltpu.CompilerParams(
            dimension_semantics=("arbitrary", "arbitrary"),
            vmem_limit_bytes=_vmem_limit(2 * d * d * 4 + 4 * rows * d * 4),
        ),
        name="adaln_mod",
    )(cs, ada_w, ada_b.reshape(depth, 1, d3))


def _lb_kernel(x_ref, o_ref):
    x = x_ref[...]
    m = jnp.max(x, axis=0, keepdims=True)
    e = jnp.exp(x - m)
    p = e / jnp.sum(e, axis=0, keepdims=True)
    depth = x.shape[0]
    rows = [p[0:1]]
    for l in range(1, depth):
        rows.append(rows[-1] + p[l:l + 1])
    cs = jnp.concatenate(rows, axis=0)
    o_ref[...] = cs - cs[0:1]


def _lower_bounds(hg_lb_logits):
    depth = hg_lb_logits.shape[0]
    flat = hg_lb_logits.reshape(depth, -1).astype(F32)
    return pl.pallas_call(
        _lb_kernel,
        out_shape=jax.ShapeDtypeStruct(flat.shape, F32),
        name="hgrn_lower_bounds",
    )(flat)


def _entry_kernel(x_ref, c_ref, modb_ref, modc_ref, g_ref, h_ref, u_ref, *, lat_tiles):
    is_lat = pl.program_id(1) < lat_tiles
    rows = jnp.where(is_lat, x_ref[0], c_ref[0]).astype(F32)
    h_ref[0] = rows
    u_ref[0] = _mod_norm(rows, g_ref[...], modb_ref[0], modc_ref[0], is_lat)


def _entry(x, ctx, mod3, norm_g, tm):
    bsz, n_lat, d = x.shape
    n = n_lat + ctx.shape[1]
    lat_tiles = n_lat // tm
    ctx_row = bsz
    tok = pl.BlockSpec((1, tm, d), lambda b, j: (b, j, 0))
    return pl.pallas_call(
        functools.partial(_entry_kernel, lat_tiles=lat_tiles),
        out_shape=[jax.ShapeDtypeStruct((bsz, n, d), F32), jax.ShapeDtypeStruct((bsz, n, d), BF16)],
        grid=(bsz, n // tm),
        in_specs=[
            pl.BlockSpec((1, tm, d), lambda b, j: (b, jnp.minimum(j, lat_tiles - 1), 0)),
            pl.BlockSpec((1, tm, d), lambda b, j: (b, jnp.maximum(j - lat_tiles, 0), 0)),
            pl.BlockSpec((1, 1, 3 * d), lambda b, j: (b, 0, 0)),
            pl.BlockSpec((1, 1, 3 * d), lambda b, j: (ctx_row, 0, 0)),
            pl.BlockSpec((1, d), lambda b, j: (0, 0)),
        ],
        out_specs=[tok, tok],
        compiler_params=pltpu.CompilerParams(
            dimension_semantics=("arbitrary", "arbitrary"),
            vmem_limit_bytes=_vmem_limit(2 * tm * d * (4 + 4 + 4 + 2) + 4 * tm * d * 4),
        ),
        name="entry_norm",
    )(x, ctx, mod3, mod3, norm_g)


def _mla_kernel(u_ref, w_ref, gq_ref, gkv_ref, wuq_ref, wukn_ref, wvt_ref, cos_ref, sin_ref, out_ref,
                q_s, k_s, vt_s, g_s, ot_s, st_s, *, n_lat, n_out, q_rank, kv_rank):
    n = u_ref.shape[1]
    n_ctx = n - n_lat
    hs = MLA_HEADS * MLA_SLOT
    width = MLA_HEADS * MLA_V
    c_kv = q_rank
    c_kr = q_rank + kv_rank
    c_krs = c_kr + MLA_SLOT
    c_g = c_krs + MLA_SLOT

    proj_tile = ROW_TILE
    for t in range(n // proj_tile):
        rows = slice(t * proj_tile, (t + 1) * proj_tile)
        z = _dot(u_ref[0, rows, :], w_ref[...])
        cos_t = cos_ref[rows, :]
        sin_t = sin_ref[rows, :]
        cos8 = jnp.concatenate([cos_t] * MLA_HEADS, axis=1)
        sin8 = jnp.concatenate([sin_t] * MLA_HEADS, axis=1)
        ckv = z[:, c_kv:c_kr]
        kvn = (ckv * lax.rsqrt(jnp.mean(ckv * ckv, axis=-1, keepdims=True) + RMS_EPS) * gkv_ref[...]).astype(BF16)
        kn = _dot(kvn, wukn_ref[...])
        kr = z[:, c_kr:c_krs] * cos_t + z[:, c_krs:c_g] * sin_t
        k = (kn + jnp.concatenate([kr] * MLA_HEADS, axis=1)).astype(BF16)
        for h in range(MLA_HEADS):
            k_s[h, rows, :] = k[:, h * MLA_SLOT:(h + 1) * MLA_SLOT]
        vt = _dot_nt(wvt_ref[...], kvn).astype(BF16)
        for h in range(MLA_HEADS):
            vt_s[h, :, rows] = vt[h * MLA_V:(h + 1) * MLA_V, :]
        if t * proj_tile < n_out:
            cq = z[:, 0:q_rank]
            cqn = (cq * lax.rsqrt(jnp.mean(cq * cq, axis=-1, keepdims=True) + RMS_EPS) * gq_ref[...]).astype(BF16)
            q2 = _dot(cqn, wuq_ref[...])
            q = ((q2[:, 0:hs] * cos8 + q2[:, hs:2 * hs] * sin8) * MLA_SCALE_LOG2).astype(BF16)
            for h in range(MLA_HEADS):
                q_s[h, rows, :] = q[:, h * MLA_SLOT:(h + 1) * MLA_SLOT]
            g_s[rows, :] = _silu(z[:, c_g:c_g + width])

    def key_block(nk):
        return MLA_KEY_BLOCK if nk % MLA_KEY_BLOCK == 0 else nk

    def scores_block(h, r0, k0, kb, kbs, m):
        st = _dot_nt(k_s[h, k0 + kb:k0 + kb + kbs, :], q_s[h, pl.ds(r0, ROW_TILE), :])
        st_s[h % 2, kb:kb + kbs, :] = st
        bm = jnp.max(st, axis=0, keepdims=True)
        return bm if m is None else jnp.maximum(m, bm)

    def scores(h, r0, k0, nk):
        m = None
        kbs = key_block(nk)
        for kb in range(0, nk, kbs):
            m = scores_block(h, r0, k0, kb, kbs, m)
        return m

    def attend(r0, k0, nk, m0, r0_next):
        rows = pl.ds(r0, ROW_TILE)
        kbs = key_block(nk)
        m_next = m0
        for h in range(MLA_HEADS):
            m = m_next
            m_next = None
            l = None
            ot = None
            for kb in range(0, nk, kbs):
                if h + 1 < MLA_HEADS:
                    m_next = scores_block(h + 1, r0, k0, kb, kbs, m_next)
                elif r0_next is not None:
                    m_next = scores_block(0, r0_next, k0, kb, kbs, m_next)
                p = jnp.exp2(st_s[h % 2, kb:kb + kbs, :] - m)
                bl = jnp.sum(p, axis=0, keepdims=True)
                l = bl if l is None else l + bl
                bo = _dot(vt_s[h, :, k0 + kb:k0 + kb + kbs], p.astype(BF16))
                ot = bo if ot is None else ot + bo
            ot_s[h] = ot * (1.0 / l)
        o = jnp.concatenate([ot_s[h] for h in range(MLA_HEADS)], axis=0).T
        out_ref[0, rows, :] = (o * g_s[rows, :]).astype(BF16)
        return m_next

    lat_tiles = n_lat // ROW_TILE

    def lat_tile(t, m0):
        r0 = pl.multiple_of(t * ROW_TILE, ROW_TILE)
        r0_next = pl.multiple_of(jnp.minimum(t + 1, lat_tiles - 1) * ROW_TILE, ROW_TILE)
        return attend(r0, 0, n, m0, r0_next)

    lax.fori_loop(0, lat_tiles, lat_tile, scores(0, 0, 0, n))
    for t in range((n_out - n_lat) // ROW_TILE):
        r0 = n_lat + t * ROW_TILE
        attend(r0, n_lat, n_ctx, scores(0, r0, n_lat, n_ctx), None)


def _mla(u, w_mla, gq, gkv, wuq2, wukn, wvt, cos_t, sin_t, n_lat, n_out):
    bsz, n, d = u.shape
    q_rank, kv_rank = gq.shape[-1], gkv.shape[-1]
    width = MLA_HEADS * MLA_V
    hs = MLA_HEADS * MLA_SLOT
    est = (2 * n * d * 2 + 2 * w_mla.size * 2 + 2 * (wuq2.size + wukn.size + wvt.size) * 2
           + 4 * n * MLA_SLOT * 4 + 2 * n_out * width * 2
           + 2 * n * hs * 2 + n * width * 2 + n * width * 4 + width * ROW_TILE * 4
           + 2 * n * ROW_TILE * 4 + 2 * n * ROW_TILE * 4)
    full = lambda a: pl.BlockSpec(a.shape, lambda b: (0,) * a.ndim)
    return pl.pallas_call(
        functools.partial(_mla_kernel, n_lat=n_lat, n_out=n_out, q_rank=q_rank, kv_rank=kv_rank),
        out_shape=jax.ShapeDtypeStruct((bsz, n_out, width), BF16),
        grid=(bsz,),
        in_specs=[
            pl.BlockSpec((1, n, d), lambda b: (b, 0, 0)),
            full(w_mla), full(gq), full(gkv), full(wuq2), full(wukn), full(wvt), full(cos_t), full(sin_t),
        ],
        out_specs=pl.BlockSpec((1, n_out, width), lambda b: (b, 0, 0)),
        scratch_shapes=[
            pltpu.VMEM((MLA_HEADS, n, MLA_SLOT), BF16),
            pltpu.VMEM((MLA_HEADS, n, MLA_SLOT), BF16),
            pltpu.VMEM((MLA_HEADS, MLA_V, n), BF16),
            pltpu.VMEM((n, width), F32),
            pltpu.VMEM((MLA_HEADS, MLA_V, ROW_TILE), F32),
            pltpu.VMEM((2, n, ROW_TILE), F32),
        ],
        compiler_params=pltpu.CompilerParams(
            dimension_semantics=("arbitrary",), vmem_limit_bytes=_vmem_limit(est)),
        name="mla_mixer",
    )(u, w_mla, gq, gkv, wuq2, wukn, wvt, cos_t, sin_t)


def _hgrn_kernel(u_ref, w_ref, lb_ref, gn_ref, out_ref,
                 o_s, st_s, q_s, v_s, gate_s, k_s, b_s, dt_s, zf_s, tot_s, *, n_lat, n_out):
    n = u_ref.shape[1]
    kw = HG_HEADS * HG_DK
    vw = HG_HEADS * HG_DV
    c_f = kw + vw
    c_gate = c_f + 2 * kw
    n_tiles = n // ROW_TILE
    lat_tiles = n_lat // ROW_TILE
    ctx_tiles = n_tiles - lat_tiles
    n_chunk = ROW_TILE // HG_CHUNK
    n_grp = kw // HG_GROUP
    heads_per_grp = HG_GROUP // HG_DK
    stack = heads_per_grp * HG_CHUNK

    ti = lax.broadcasted_iota(jnp.int32, (HG_CHUNK + 16, HG_CHUNK), 0)
    si = lax.broadcasted_iota(jnp.int32, (HG_CHUNK + 16, HG_CHUNK), 1)
    tri = tuple(jnp.where(((ti < HG_CHUNK) & m) | (ti == HG_CHUNK), 1.0, 0.0).astype(BF16)
                for m in (si <= ti, si >= ti))
    hr = lax.broadcasted_iota(jnp.int32, (stack, HG_GROUP), 0) // HG_CHUNK
    hc = lax.broadcasted_iota(jnp.int32, (stack, HG_GROUP), 1) // HG_DK
    head_mask = jnp.where(hr == hc, 1.0, 0.0).astype(BF16)
    hl = lax.broadcasted_iota(jnp.int32, (1, HG_GROUP // 2), 1) // HG_DK
    half_mask = (jnp.where(hl == 0, 1.0, 0.0), jnp.where(hl == 1, 1.0, 0.0))
    at = lax.broadcasted_iota(jnp.int32, (HG_CHUNK, stack), 0)
    as_ = lax.broadcasted_iota(jnp.int32, (HG_CHUNK, stack), 1) % HG_CHUNK
    causal = (as_ <= at, as_ >= at)
    hq = HG_GROUP // 2
    zq = jnp.zeros((hq, hq), BF16)
    ref_row = (HG_CHUNK // 2 - 1, HG_CHUNK // 2)
    end_row = (HG_CHUNK - 1, 0)

    lb_all = lb_ref[...]

    qvg_tile = 768 if n % 768 == 0 else ROW_TILE

    def qvg(t):
        rows = slice(t * qvg_tile, (t + 1) * qvg_tile)
        ut = u_ref[0, rows, :]
        qv = _dot(ut, w_ref[:, 0:c_f])
        q_s[rows, :] = qv[:, 0:kw]
        v_s[rows, :] = qv[:, kw:c_f].astype(BF16)
        gate_s[rows, :] = _silu(_dot(ut, w_ref[:, c_gate:c_gate + vw])).astype(BF16)

    def tiles_of(step):
        t_fwd = jnp.where(step < ctx_tiles, lat_tiles + step, step - ctx_tiles)
        t_bwd = n_tiles - 1 - step
        return (t_fwd, t_bwd)

    def precompute(step, slot):
        tl = tiles_of(step)

        def logits():
            for d in range(2):
                rows = pl.ds(pl.multiple_of(tl[d] * ROW_TILE, ROW_TILE), ROW_TILE)
                zf_s[d] = _dot(u_ref[0, rows, :], w_ref[:, c_f + d * kw:c_f + (d + 1) * kw])

        def gates(cl):
            r = slice(cl * HG_CHUNK, (cl + 1) * HG_CHUNK)
            for d in range(2):
                idx = slot * 2 + d
                zf = zf_s[d, r, :]
                lb = lb_all[:, d * kw:(d + 1) * kw]
                e = jnp.exp(-jnp.abs(zf))
                log_sig = jnp.minimum(zf, 0.0) - jnp.log(1.0 + e)
                a = jnp.log(lb)
                c = jnp.log(1.0 - lb) + log_sig
                log_f = jnp.maximum(a, c) + jnp.log(1.0 + jnp.exp(-jnp.abs(a - c)))
                rcp = 1.0 / (1.0 + e)
                k_s[idx, r, :] = (1.0 - lb) * jnp.where(zf >= 0.0, e * rcp, rcp)
                g_hi, g_lo = _split2(log_f)
                cum = (_dot(tri[d], g_hi) + _dot(tri[d], g_lo)) * LOG2_E
                b_s[idx, r, :] = cum[0:HG_CHUNK]
                tot_s[d, cl:cl + 1, :] = cum[HG_CHUNK:HG_CHUNK + 1]

        def decays():
            for d in range(2):
                dt_s[slot * 2 + d] = jnp.exp2(tot_s[d]).T

        return [logits] + [functools.partial(gates, cl) for cl in range(n_chunk)] + [decays]

    def chunks(step, slot, between=()):
        tl = tiles_of(step)
        between = list(between)
        for cidx in range(n_chunk):
            for d in range(2):
                idx = slot * 2 + d
                cl = cidx if d == 0 else n_chunk - 1 - cidx
                c0 = cl * HG_CHUNK
                rows = slice(c0, c0 + HG_CHUNK)
                grow = pl.ds(pl.multiple_of(tl[d] * ROW_TILE + c0, HG_CHUNK), HG_CHUNK)
                for g in range(n_grp):
                    lanes = slice(g * HG_GROUP, (g + 1) * HG_GROUP)
                    q = q_s[grow, lanes]
                    v = v_s[grow, lanes]
                    k = k_s[idx, rows, lanes]
                    b = b_s[idx, rows, lanes]
                    b_ref = b_s[idx, c0 + ref_row[d]:c0 + ref_row[d] + 1, lanes]
                    b_end = b_s[idx, c0 + end_row[d]:c0 + end_row[d] + 1, lanes]
                    qd = (q * jnp.exp2(b - b_ref)).astype(BF16)
                    qs = (q * jnp.exp2(b)).astype(BF16)
                    kd = (k * jnp.exp2(b_ref - b)).astype(BF16)
                    kr = k * jnp.exp2(b_end - b)
                    kd_st = jnp.concatenate([kd] * heads_per_grp, axis=0) * head_mask
                    v_st = jnp.concatenate([v] * heads_per_grp, axis=0) * head_mask
                    kr_fold = jnp.concatenate(
                        [kr[:, (j // 2) * hq:(j // 2 + 1) * hq] * half_mask[j % 2] for j in range(heads_per_grp)],
                        axis=0)
                    kr_t = kr_fold.T.astype(BF16)
                    amat = jnp.where(causal[d], _dot_nt(qd, kd_st), 0.0).astype(BF16)
                    res = _dot(jnp.concatenate([kr_t, amat], axis=0), v_st)
                    sidx = d * n_grp + g
                    dtc = dt_s[idx, lanes, cl:cl + 1]
                    s0 = st_s[sidx, 0]
                    s1 = st_s[sidx, 1]
                    s_bf = jnp.concatenate(
                        [jnp.concatenate([s0.astype(BF16), zq], axis=1),
                         jnp.concatenate([zq, s1.astype(BF16)], axis=1)], axis=0)
                    o_s[d, grow, lanes] = res[hq:] + _dot(qs, s_bf)
                    st_s[sidx, 0] = s0 * dtc[0:hq] + res[0:hq, 0:hq]
                    st_s[sidx, 1] = s1 * dtc[hq:HG_GROUP] + res[0:hq, hq:HG_GROUP]
            if between:
                between.pop(0)()
        for piece in between:
            piece()

    hi_ = lax.broadcasted_iota(jnp.int32, (vw, vw), 0) // HG_DV
    hj_ = lax.broadcasted_iota(jnp.int32, (vw, vw), 1) // HG_DV
    head_sum = jnp.where(hi_ == hj_, 1.0, 0.0).astype(BF16)

    def finish(t):
        rows = slice(t * ROW_TILE, (t + 1) * ROW_TILE)
        o = o_s[0, rows, :] + o_s[1, rows, :]
        sq_hi, sq_lo = _split2(o * o)
        ms = (_dot(sq_hi, head_sum) + _dot(sq_lo, head_sum)) * (1.0 / HG_DV)
        y = o * lax.rsqrt(ms + RMS_EPS) * gn_ref[...]
        out_ref[0, rows, :] = (y * gate_s[rows, :].astype(F32)).astype(BF16)

    out_tiles = n_out // ROW_TILE
    late_tiles = sorted({0, lat_tiles - 1} & set(range(out_tiles)))
    early_tiles = [t for t in range(out_tiles) if t not in late_tiles]

    def step(i, slot, has_next):
        if has_next:
            pieces = precompute(i + 1, 1 - slot)
            pieces[0]()
            chunks(i, slot, pieces[1:])
        else:
            chunks(i, slot, [functools.partial(finish, t) for t in early_tiles])

    st_s[...] = jnp.zeros_like(st_s)
    tot_s[...] = jnp.zeros_like(tot_s)
    first = precompute(0, 0)
    n_qvg = n // qvg_tile
    per = -(-len(first) // n_qvg)
    for t in range(n_qvg):
        qvg(t)
        for piece in first[t * per:(t + 1) * per]:
            piece()

    def step_pair(j, carry):
        step(2 * j, 0, True)
        step(2 * j + 1, 1, True)
        return carry

    n_pairs = (n_tiles - 1) // 2
    lax.fori_loop(0, n_pairs, step_pair, 0)
    if 2 * n_pairs == n_tiles - 1:
        step(n_tiles - 1, 0, False)
    else:
        step(n_tiles - 2, 0, True)
        step(n_tiles - 1, 1, False)
    for t in late_tiles:
        finish(t)


def _hgrn(u, w_hg, lb, gn, n_lat, n_out):
    bsz, n, d = u.shape
    kw = HG_HEADS * HG_DK
    vw = HG_HEADS * HG_DV
    n_streams = 2 * (kw // HG_GROUP)
    est = (2 * n * d * 2 + 2 * w_hg.size * 2 + 2 * n_out * vw * 2
           + 2 * n * vw * 4 + n_streams * HG_GROUP * HG_GROUP * 2 + n * kw * 4 + n * 2 * vw * 2
           + 8 * ROW_TILE * kw * 4 + 4 * kw * V7X_LANES * 4 + 6 * ROW_TILE * kw * 4)
    full = lambda a: pl.BlockSpec(a.shape, lambda b: (0,) * a.ndim)
    return pl.pallas_call(
        functools.partial(_hgrn_kernel, n_lat=n_lat, n_out=n_out),
        out_shape=jax.ShapeDtypeStruct((bsz, n_out, vw), BF16),
        grid=(bsz,),
        in_specs=[pl.BlockSpec((1, n, d), lambda b: (b, 0, 0)), full(w_hg), full(lb), full(gn)],
        out_specs=pl.BlockSpec((1, n_out, vw), lambda b: (b, 0, 0)),
        scratch_shapes=[
            pltpu.VMEM((2, n, vw), F32),
            pltpu.VMEM((n_streams, 2, HG_GROUP // 2, HG_GROUP // 2), F32),
            pltpu.VMEM((n, kw), F32),
            pltpu.VMEM((n, vw), BF16),
            pltpu.VMEM((n, vw), BF16),
            pltpu.VMEM((4, ROW_TILE, kw), F32),
            pltpu.VMEM((4, ROW_TILE, kw), F32),
            pltpu.VMEM((4, kw, V7X_LANES), F32),
            pltpu.VMEM((2, ROW_TILE, kw), F32),
            pltpu.VMEM((2, V7X_LANES, kw), F32),
        ],
        compiler_params=pltpu.CompilerParams(
            dimension_semantics=("arbitrary",), vmem_limit_bytes=_vmem_limit(est)),
        name="hgrn_mixer",
    )(u, w_hg, lb, gn)


CV_PAD = 8


def _conv_kernel(u_ref, w_ref, cw_ref, cb_ref, out_ref, uu_s, coef_s, *, n_lat, n_out, tile):
    n = u_ref.shape[1]
    cw = cw_ref.shape[-1]
    out_tiles = n_out // tile
    zero = jnp.zeros((CV_PAD, cw), F32)
    uu_s[0:CV_PAD, :] = zero
    uu_s[CV_PAD + n_lat:2 * CV_PAD + n_lat, :] = zero
    uu_s[2 * CV_PAD + n:3 * CV_PAD + n, :] = zero

    def srow(r):
        return r + CV_PAD + (CV_PAD if r >= n_lat else 0)

    def pieces(t):
        r0, r1 = t * tile, (t + 1) * tile
        cand = ((r0, min(r1, n_lat)), (max(r0, n_lat), r1))
        return [(a, b) for a, b in cand if a < b]

    w0 = cw_ref[0:1, :]
    w1 = cw_ref[1:2, :]
    w2 = cw_ref[2:3, :]
    bias = cb_ref[...]

    def project(t):
        r0 = t * tile
        z = _dot(u_ref[0, r0:r0 + tile, :], w_ref[...])
        uu = z[:, 2 * cw:3 * cw] * z[:, 0:cw]
        for a, b in pieces(t):
            uu_s[srow(a):srow(a) + (b - a), :] = uu[a - r0:b - r0]
        coef_s[r0:r0 + tile, :] = z[:, cw:2 * cw] * _silu(z[:, 3 * cw:4 * cw])

    def emit(t):
        for a, b in pieces(t):
            o, m = srow(a), b - a
            conv = uu_s[o - 1:o - 1 + m, :] * w0 + uu_s[o:o + m, :] * w1 + uu_s[o + 1:o + 1 + m, :] * w2 + bias
            out_ref[0, a:b, :] = (coef_s[a:b, :] * conv).astype(BF16)

    for t in range(out_tiles):
        project(t)
        if t >= 1:
            emit(t - 1)
    emit(out_tiles - 1)


def _conv(u, w_cv, conv_w, conv_b, n_lat, n_out):
    bsz, n, d = u.shape
    cw = conv_w.shape[-1]
    tile = next(t for t in (768, 1024, ROW_TILE) if n_out % t == 0)
    est = (2 * n * d * 2 + 2 * w_cv.size * 2 + 2 * n_out * cw * 2 + (n + 3 * CV_PAD) * cw * 4 + n * cw * 4
           + 2 * tile * 4 * cw * 4)
    full = lambda a: pl.BlockSpec(a.shape, lambda b: (0,) * a.ndim)
    return pl.pallas_call(
        functools.partial(_conv_kernel, n_lat=n_lat, n_out=n_out, tile=tile),
        out_shape=jax.ShapeDtypeStruct((bsz, n_out, cw), BF16),
        grid=(bsz,),
        in_specs=[pl.BlockSpec((1, n, d), lambda b: (b, 0, 0)), full(w_cv), full(conv_w), full(conv_b)],
        out_specs=pl.BlockSpec((1, n_out, cw), lambda b: (b, 0, 0)),
        scratch_shapes=[pltpu.VMEM((n + 3 * CV_PAD, cw), F32), pltpu.VMEM((n, cw), F32)],
        compiler_params=pltpu.CompilerParams(
            dimension_semantics=("arbitrary",), vmem_limit_bytes=_vmem_limit(est)),
        name="conv_mixer",
    )(u, w_cv, conv_w, conv_b)


def _merge_kernel(u_ref, ym_ref, yh_ref, yc_ref, h_ref, modb_ref, modc_ref, wg_ref, wb_ref, wo_ref,
                  nmodb_ref, nmodc_ref, ng_ref, *out_refs, n_lat, tm, last):
    d = h_ref.shape[-1]
    sub = MERGE_SUB if tm % MERGE_SUB == 0 else tm

    def project(s):
        rows = slice(s * sub, (s + 1) * sub)
        u = u_ref[0, rows, :]
        acc = None
        for i, y_ref in enumerate((ym_ref, yh_ref, yc_ref)):
            gate_i = _sigmoid(_dot(u, wg_ref[:, i * d:(i + 1) * d]))
            term = gate_i * _dot(y_ref[0, rows, :], wb_ref[i])
            acc = term if acc is None else acc + term
        return _dot(acc.astype(BF16), wo_ref[...])

    def finish(s, out):
        rows = slice(s * sub, (s + 1) * sub)
        row = pl.program_id(1) * tm + s * sub + lax.broadcasted_iota(jnp.int32, (sub, 1), 0)
        is_lat = row < n_lat
        gate = jnp.where(is_lat, modb_ref[0][:, 2 * d:3 * d], modc_ref[0][:, 2 * d:3 * d])
        hn = h_ref[0, rows, :] + gate * out
        if last:
            o_ref, = out_refs
            o_ref[0, rows, :] = hn * lax.rsqrt(jnp.mean(hn * hn, axis=-1, keepdims=True) + RMS_EPS) * ng_ref[...]
        else:
            o_ref, un_ref = out_refs
            o_ref[0, rows, :] = hn
            un_ref[0, rows, :] = _mod_norm(hn, ng_ref[...], nmodb_ref[0], nmodc_ref[0], is_lat)

    pending = None
    for s in range(tm // sub):
        out = project(s)
        if pending is not None:
            finish(*pending)
        pending = (s, out)
    finish(*pending)


def _merge(u, ym, yh, yc, h, mod3, w_gate, w_branch, w_out, next_mod3, next_g, n_lat, tm, last):
    bsz, n, d = h.shape
    bw = ym.shape[-1]
    n_out = n_lat if last else n
    ctx_row = bsz
    tok = lambda w: pl.BlockSpec((1, tm, w), lambda b, j: (b, j, 0))
    modb = pl.BlockSpec((1, 1, 3 * d), lambda b, j: (b, 0, 0))
    modc = pl.BlockSpec((1, 1, 3 * d), lambda b, j: (ctx_row, 0, 0))
    full = lambda a: pl.BlockSpec(a.shape, lambda b, j: (0,) * a.ndim)
    once = lambda a: pl.BlockSpec(a.shape, lambda b, j: (0,) * a.ndim, pipeline_mode=pl.Buffered(1))
    est = (2 * tm * (d * 2 + 3 * bw * 2 + 2 * d * 4 + d * 2) + (w_gate.size + w_branch.size + w_out.size) * 2
           + 6 * tm * d * 4)
    out_shape = [jax.ShapeDtypeStruct((bsz, n_out, d), F32)]
    out_specs = [tok(d)]
    if not last:
        out_shape.append(jax.ShapeDtypeStruct((bsz, n_out, d), BF16))
        out_specs.append(tok(d))
    return pl.pallas_call(
        functools.partial(_merge_kernel, n_lat=n_lat, tm=tm, last=last),
        out_shape=out_shape,
        grid=(bsz, n_out // tm),
        in_specs=[
            tok(d), tok(bw), tok(bw), tok(bw), tok(d), modb, modc,
            once(w_gate), once(w_branch), once(w_out), modb, modc, full(next_g),
        ],
        out_specs=out_specs,
        compiler_params=pltpu.CompilerParams(
            dimension_semantics=("arbitrary", "arbitrary"), vmem_limit_bytes=_vmem_limit(est)),
        name="merge_out",
    )(u, ym, yh, yc, h, mod3, mod3, w_gate, w_branch, w_out, next_mod3, next_mod3, next_g)


def _rope_tables(n_lat, n_ctx):
    pairs = MLA_ROPE // 4
    rows = n_lat // GRID_W
    row_id = np.repeat(np.arange(rows, dtype=np.float32), GRID_W)
    col_id = np.tile(np.arange(GRID_W, dtype=np.float32), rows)
    inv_freq = jnp.power(ROPE_BASE, -jnp.arange(pairs, dtype=F32) / pairs)
    ang = jnp.stack([row_id[:, None] * inv_freq, col_id[:, None] * inv_freq], axis=1)
    ang = jnp.broadcast_to(ang[:, :, None, :], (n_lat, 2, 2, pairs)).reshape(n_lat, MLA_ROPE)
    pad_hi = MLA_SLOT - MLA_NOPE - MLA_ROPE
    cos = jnp.concatenate([jnp.ones((n_lat, MLA_NOPE), F32), jnp.cos(ang), jnp.ones((n_lat, pad_hi), F32)], axis=1)
    sin = jnp.concatenate([jnp.zeros((n_lat, MLA_NOPE), F32), jnp.sin(ang), jnp.zeros((n_lat, pad_hi), F32)], axis=1)
    cos = jnp.concatenate([cos, jnp.ones((n_ctx, MLA_SLOT), F32)], axis=0)
    sin = jnp.concatenate([sin, jnp.zeros((n_ctx, MLA_SLOT), F32)], axis=0)
    return cos, sin


def _rotate_half_cols(w):
    pairs = MLA_ROPE // 4
    ws = w.reshape(w.shape[:-1] + (2, 2, pairs))
    return jnp.stack([-ws[..., 1, :], ws[..., 0, :]], axis=-2).reshape(w.shape)


def _mla_weights(w_in_l, w_uq_l, w_ukv_l, q_rank, kv_rank):
    d = w_in_l.shape[0]
    width = MLA_HEADS * MLA_V
    pad_hi = MLA_SLOT - MLA_NOPE - MLA_ROPE
    c_kr = q_rank + kv_rank
    w_kr = w_in_l[:, c_kr:c_kr + MLA_ROPE]
    slot = lambda w: jnp.concatenate([jnp.zeros((d, MLA_NOPE), w.dtype), w, jnp.zeros((d, pad_hi), w.dtype)], axis=1)
    w_mla = jnp.concatenate([w_in_l[:, 0:c_kr], slot(w_kr), slot(_rotate_half_cols(w_kr)),
                             w_in_l[:, c_kr + MLA_ROPE:c_kr + MLA_ROPE + width]], axis=1)
    uq = w_uq_l.reshape(q_rank, MLA_HEADS, MLA_NOPE + MLA_ROPE)
    uq_n, uq_r = uq[..., :MLA_NOPE], uq[..., MLA_NOPE:]
    zn = jnp.zeros_like(uq_n)
    zp = jnp.zeros((q_rank, MLA_HEADS, pad_hi), uq.dtype)
    main = jnp.concatenate([uq_n, uq_r, zp], axis=-1).reshape(q_rank, MLA_HEADS * MLA_SLOT)
    part = jnp.concatenate([zn, _rotate_half_cols(uq_r), zp], axis=-1).reshape(q_rank, MLA_HEADS * MLA_SLOT)
    wuq2 = jnp.concatenate([main, part], axis=1)
    ukv = w_ukv_l.reshape(kv_rank, MLA_HEADS, MLA_NOPE + MLA_V)
    kn = ukv[..., :MLA_NOPE]
    wukn = jnp.concatenate([kn, jnp.zeros((kv_rank, MLA_HEADS, MLA_SLOT - MLA_NOPE), kn.dtype)], axis=-1)
    wukn = wukn.reshape(kv_rank, MLA_HEADS * MLA_SLOT)
    wvt = ukv[..., MLA_NOPE:].reshape(kv_rank, width).T
    return w_mla, wuq2, wukn, wvt


def kernel(x, c, ctx, c_ctx, ada_w, ada_b, norm_g, w_in, mla_q_norm_g, mla_kv_norm_g, mla_w_uq, mla_w_ukv,
           hg_lb_logits, hg_norm_g, conv_w, conv_b, w_branch, w_out, final_norm_g):
    bsz, n_lat, d = x.shape
    n_ctx = ctx.shape[1]
    n = n_lat + n_ctx
    depth = w_in.shape[0]
    q_rank = mla_q_norm_g.shape[-1]
    kv_rank = mla_kv_norm_g.shape[-1]
    width = MLA_HEADS * MLA_V
    kw = HG_HEADS * HG_DK
    vw = HG_HEADS * HG_DV
    cw = conv_w.shape[-1]
    assert n_lat % ROW_TILE == 0 and n_ctx % ROW_TILE == 0 and n_lat % GRID_W == 0
    assert width == vw == cw == w_branch.shape[2]

    pad = (-(bsz + 1)) % 8
    cs = jnp.concatenate([c, c_ctx[None, :], jnp.zeros((pad, d), c.dtype)], axis=0).astype(F32)
    mod = _modulation(cs, ada_w.astype(F32), ada_b.astype(F32))
    mod3 = [mod[l].reshape(mod.shape[1], 1, 3 * d) for l in range(depth)]
    lb_all = _lower_bounds(hg_lb_logits)
    cos_t, sin_t = _rope_tables(n_lat, n_ctx)

    c_hg = q_rank + kv_rank + MLA_ROPE + width
    c_cv = c_hg + 2 * kw + vw + kw + vw
    c_gate = c_cv + 4 * cw
    w_in_b = w_in.astype(BF16)
    w_branch_b = w_branch.astype(BF16)
    w_out_b = w_out.astype(BF16)
    norm_gs = [norm_g[l].reshape(1, d).astype(F32) for l in range(depth)]
    final_g = final_norm_g.reshape(1, d).astype(F32)

    tm_merge = 768 if n % 768 == 0 else ROW_TILE
    tm_last = 1024 if n_lat % 1024 == 0 else ROW_TILE
    h, u = _entry(x, ctx, mod3[0], norm_gs[0], ROW_TILE)
    for l in range(depth):
        last = l == depth - 1
        n_out = n_lat if last else n
        w_mla, wuq2, wukn, wvt = _mla_weights(w_in_b[l], mla_w_uq[l].astype(BF16), mla_w_ukv[l].astype(BF16),
                                              q_rank, kv_rank)
        y_mla = _mla(u, w_mla, mla_q_norm_g[l].reshape(1, q_rank).astype(F32),
                     mla_kv_norm_g[l].reshape(1, kv_rank).astype(F32), wuq2, wukn, wvt, cos_t, sin_t, n_lat, n_out)
        y_hg = _hgrn(u, w_in_b[l, :, c_hg:c_cv], lb_all[l].reshape(1, 2 * kw),
                     hg_norm_g[l].reshape(1, vw).astype(F32), n_lat, n_out)
        y_cv = _conv(u, w_in_b[l, :, c_cv:c_gate], conv_w[l].astype(F32), conv_b[l].reshape(1, cw).astype(F32),
                     n_lat, n_out)
        if last:
            h, = _merge(u, y_mla, y_hg, y_cv, h, mod3[l], w_in_b[l, :, c_gate:], w_branch_b[l], w_out_b[l],
                        mod3[l], final_g, n_lat, tm_last, True)
        else:
            h, u = _merge(u, y_mla, y_hg, y_cv, h, mod3[l], w_in_b[l, :, c_gate:], w_branch_b[l], w_out_b[l],
                          mod3[l + 1], norm_gs[l + 1], n_lat, tm_merge, False)
    return h
```

```python
import functools

import numpy as np
import jax
import jax.numpy as jnp
from jax import lax
from jax.experimental import pallas as pl
from jax.experimental.pallas import tpu as pltpu

F32 = jnp.float32
BF16 = jnp.bfloat16

RMS_EPS = 1e-6
GRID_W = 64
ROPE_BASE = 10000.0
MLA_HEADS = 8
MLA_NOPE = 64
MLA_ROPE = 32
MLA_V = 64
MLA_SLOT = 128
MLA_KEY_BLOCK = 768
MLA_SCALE = (MLA_NOPE + MLA_ROPE) ** -0.5
LOG2_E = float(np.log2(np.e))
MLA_SCALE_LOG2 = MLA_SCALE * LOG2_E
HG_HEADS = 8
HG_DK = 64
HG_DV = 64
HG_GROUP = 256
HG_CHUNK = 32
CV_K = 3

V7X_VMEM_BYTES = 64 * 1024 * 1024
V7X_LANES = 128
ROW_TILE = 256
MERGE_SUB = 256


def _vmem_limit(nbytes):
    return int(min(nbytes + (12 << 20), V7X_VMEM_BYTES - (6 << 20)))


def _dot(a, b):
    return jnp.dot(a, b, preferred_element_type=F32)


def _dot_nt(a, b):
    return lax.dot_general(a, b, (((1,), (1,)), ((), ())), preferred_element_type=F32)


def _sigmoid(x):
    return 1.0 / (1.0 + jnp.exp(-x))


def _silu(x):
    return x * _sigmoid(x)


def _split2(x):
    hi = x.astype(BF16)
    lo = (x - hi.astype(F32)).astype(BF16)
    return hi, lo


def _mod_norm(x, g, mod_b, mod_c, is_lat):
    d = x.shape[-1]
    y = x * lax.rsqrt(jnp.mean(x * x, axis=-1, keepdims=True) + RMS_EPS) * g
    shift = jnp.where(is_lat, mod_b[:, 0:d], mod_c[:, 0:d])
    scale = jnp.where(is_lat, mod_b[:, d:2 * d], mod_c[:, d:2 * d])
    return (y * (1.0 + scale) + shift).astype(BF16)


def _mod_kernel(c_ref, w_ref, b_ref, o_ref):
    s = _silu(c_ref[...])
    o_ref[0] = jnp.dot(s, w_ref[0], preferred_element_type=F32, precision=lax.Precision.HIGHEST) + b_ref[0]


def _modulation(cs, ada_w, ada_b):
    depth, d, d3 = ada_w.shape
    rows = cs.shape[0]
    nb = d3 // d
    return pl.pallas_call(
        _mod_kernel,
        out_shape=jax.ShapeDtypeStruct((depth, rows, d3), F32),
        grid=(depth, nb),
        in_specs=[
            pl.BlockSpec((rows, d), lambda l, j: (0, 0)),
            pl.BlockSpec((1, d, d), lambda l, j: (l, 0, j)),
            pl.BlockSpec((1, 1, d), lambda l, j: (l, 0, j)),
        ],
        out_specs=pl.BlockSpec((1, rows, d), lambda l, j: (l, 0, j)),
        compiler_params=pltpu.CompilerParams(
            dimension_semantics=("arbitrary", "arbitrary"),
            vmem_limit_bytes=_vmem_limit(2 * d * d * 4 + 4 * rows * d * 4),
        ),
        name="adaln_mod",
    )(cs, ada_w, ada_b.reshape(depth, 1, d3))


def _lb_kernel(x_ref, o_ref):
    x = x_ref[...]
    m = jnp.max(x, axis=0, keepdims=True)
    e = jnp.exp(x - m)
    p = e / jnp.sum(e, axis=0, keepdims=True)
    depth = x.shape[0]
    rows = [p[0:1]]
    for l in range(1, depth):
        rows.append(rows[-1] + p[l:l + 1])
    cs = jnp.concatenate(rows, axis=0)
    o_ref[...] = cs - cs[0:1]


def _lower_bounds(hg_lb_logits):
    depth = hg_lb_logits.shape[0]
    flat = hg_lb_logits.reshape(depth, -1).astype(F32)
    return pl.pallas_call(
        _lb_kernel,
        out_shape=jax.ShapeDtypeStruct(flat.shape, F32),
        name="hgrn_lower_bounds",
    )(flat)


def _entry_kernel(x_ref, c_ref, modb_ref, modc_ref, g_ref, h_ref, u_ref, *, lat_tiles):
    is_lat = pl.program_id(1) < lat_tiles
    rows = jnp.where(is_lat, x_ref[0], c_ref[0]).astype(F32)
    h_ref[0] = rows
    u_ref[0] = _mod_norm(rows, g_ref[...], modb_ref[0], modc_ref[0], is_lat)


def _entry(x, ctx, mod3, norm_g, tm):
    bsz, n_lat, d = x.shape
    n = n_lat + ctx.shape[1]
    lat_tiles = n_lat // tm
    ctx_row = bsz
    tok = pl.BlockSpec((1, tm, d), lambda b, j: (b, j, 0))
    return pl.pallas_call(
        functools.partial(_entry_kernel, lat_tiles=lat_tiles),
        out_shape=[jax.ShapeDtypeStruct((bsz, n, d), F32), jax.ShapeDtypeStruct((bsz, n, d), BF16)],
        grid=(bsz, n // tm),
        in_specs=[
            pl.BlockSpec((1, tm, d), lambda b, j: (b, jnp.minimum(j, lat_tiles - 1), 0)),
            pl.BlockSpec((1, tm, d), lambda b, j: (b, jnp.maximum(j - lat_tiles, 0), 0)),
            pl.BlockSpec((1, 1, 3 * d), lambda b, j: (b, 0, 0)),
            pl.BlockSpec((1, 1, 3 * d), lambda b, j: (ctx_row, 0, 0)),
            pl.BlockSpec((1, d), lambda b, j: (0, 0)),
        ],
        out_specs=[tok, tok],
        compiler_params=pltpu.CompilerParams(
            dimension_semantics=("arbitrary", "arbitrary"),
            vmem_limit_bytes=_vmem_limit(2 * tm * d * (4 + 4 + 4 + 2) + 4 * tm * d * 4),
        ),
        name="entry_norm",
    )(x, ctx, mod3, mod3, norm_g)


def _mla_kernel(u_ref, w_ref, gq_ref, gkv_ref, wuq_ref, wukn_ref, wvt_ref, cos_ref, sin_ref, out_ref,
                q_s, k_s, vt_s, g_s, ot_s, st_s, *, n_lat, n_out, q_rank, kv_rank):
    n = u_ref.shape[1]
    n_ctx = n - n_lat
    hs = MLA_HEADS * MLA_SLOT
    width = MLA_HEADS * MLA_V
    c_kv = q_rank
    c_kr = q_rank + kv_rank
    c_krs = c_kr + MLA_SLOT
    c_g = c_krs + MLA_SLOT

    proj_tile = ROW_TILE
    for t in range(n // proj_tile):
        rows = slice(t * proj_tile, (t + 1) * proj_tile)
        z = _dot(u_ref[0, rows, :], w_ref[...])
        cos_t = cos_ref[rows, :]
        sin_t = sin_ref[rows, :]
        cos8 = jnp.concatenate([cos_t] * MLA_HEADS, axis=1)
        sin8 = jnp.concatenate([sin_t] * MLA_HEADS, axis=1)
        ckv = z[:, c_kv:c_kr]
        kvn = (ckv * lax.rsqrt(jnp.mean(ckv * ckv, axis=-1, keepdims=True) + RMS_EPS) * gkv_ref[...]).astype(BF16)
        kn = _dot(kvn, wukn_ref[...])
        kr = z[:, c_kr:c_krs] * cos_t + z[:, c_krs:c_g] * sin_t
        k = (kn + jnp.concatenate([kr] * MLA_HEADS, axis=1)).astype(BF16)
        for h in range(MLA_HEADS):
            k_s[h, rows, :] = k[:, h * MLA_SLOT:(h + 1) * MLA_SLOT]
        vt = _dot_nt(wvt_ref[...], kvn).astype(BF16)
        for h in range(MLA_HEADS):
            vt_s[h, :, rows] = vt[h * MLA_V:(h + 1) * MLA_V, :]
        if t * proj_tile < n_out:
            cq = z[:, 0:q_rank]
            cqn = (cq * lax.rsqrt(jnp.mean(cq * cq, axis=-1, keepdims=True) + RMS_EPS) * gq_ref[...]).astype(BF16)
            q2 = _dot(cqn, wuq_ref[...])
            q = ((q2[:, 0:hs] * cos8 + q2[:, hs:2 * hs] * sin8) * MLA_SCALE_LOG2).astype(BF16)
            for h in range(MLA_HEADS):
                q_s[h, rows, :] = q[:, h * MLA_SLOT:(h + 1) * MLA_SLOT]
            g_s[rows, :] = _silu(z[:, c_g:c_g + width])

    def key_block(nk):
        return MLA_KEY_BLOCK if nk % MLA_KEY_BLOCK == 0 else nk

    def scores_block(h, r0, k0, kb, kbs, m):
        st = _dot_nt(k_s[h, k0 + kb:k0 + kb + kbs, :], q_s[h, pl.ds(r0, ROW_TILE), :])
        st_s[h % 2, kb:kb + kbs, :] = st
        bm = jnp.max(st, axis=0, keepdims=True)
        return bm if m is None else jnp.maximum(m, bm)

    def scores(h, r0, k0, nk):
        m = None
        kbs = key_block(nk)
        for kb in range(0, nk, kbs):
            m = scores_block(h, r0, k0, kb, kbs, m)
        return m

    def attend(r0, k0, nk, m0, r0_next):
        rows = pl.ds(r0, ROW_TILE)
        kbs = key_block(nk)
        m_next = m0
        for h in range(MLA_HEADS):
            m = m_next
            m_next = None
            l = None
            ot = None
            for kb in range(0, nk, kbs):
                if h + 1 < MLA_HEADS:
                    m_next = scores_block(h + 1, r0, k0, kb, kbs, m_next)
                elif r0_next is not None:
                    m_next = scores_block(0, r0_next, k0, kb, kbs, m_next)
                p = jnp.exp2(st_s[h % 2, kb:kb + kbs, :] - m)
                bl = jnp.sum(p, axis=0, keepdims=True)
                l = bl if l is None else l + bl
                bo = _dot(vt_s[h, :, k0 + kb:k0 + kb + kbs], p.astype(BF16))
                ot = bo if ot is None else ot + bo
            ot_s[h] = ot * (1.0 / l)
        o = jnp.concatenate([ot_s[h] for h in range(MLA_HEADS)], axis=0).T
        out_ref[0, rows, :] = (o * g_s[rows, :]).astype(BF16)
        return m_next

    lat_tiles = n_lat // ROW_TILE

    def lat_tile(t, m0):
        r0 = pl.multiple_of(t * ROW_TILE, ROW_TILE)
        r0_next = pl.multiple_of(jnp.minimum(t + 1, lat_tiles - 1) * ROW_TILE, ROW_TILE)
        return attend(r0, 0, n, m0, r0_next)

    lax.fori_loop(0, lat_tiles, lat_tile, scores(0, 0, 0, n))
    for t in range((n_out - n_lat) // ROW_TILE):
        r0 = n_lat + t * ROW_TILE
        attend(r0, n_lat, n_ctx, scores(0, r0, n_lat, n_ctx), None)


def _mla(u, w_mla, gq, gkv, wuq2, wukn, wvt, cos_t, sin_t, n_lat, n_out):
    bsz, n, d = u.shape
    q_rank, kv_rank = gq.shape[-1], gkv.shape[-1]
    width = MLA_HEADS * MLA_V
    hs = MLA_HEADS * MLA_SLOT
    est = (2 * n * d * 2 + 2 * w_mla.size * 2 + 2 * (wuq2.size + wukn.size + wvt.size) * 2
           + 4 * n * MLA_SLOT * 4 + 2 * n_out * width * 2
           + 2 * n * hs * 2 + n * width * 2 + n * width * 4 + width * ROW_TILE * 4
           + 2 * n * ROW_TILE * 4 + 2 * n * ROW_TILE * 4)
    full = lambda a: pl.BlockSpec(a.shape, lambda b: (0,) * a.ndim)
    return pl.pallas_call(
        functools.partial(_mla_kernel, n_lat=n_lat, n_out=n_out, q_rank=q_rank, kv_rank=kv_rank),
        out_shape=jax.ShapeDtypeStruct((bsz, n_out, width), BF16),
        grid=(bsz,),
        in_specs=[
            pl.BlockSpec((1, n, d), lambda b: (b, 0, 0)),
            full(w_mla), full(gq), full(gkv), full(wuq2), full(wukn), full(wvt), full(cos_t), full(sin_t),
        ],
        out_specs=pl.BlockSpec((1, n_out, width), lambda b: (b, 0, 0)),
        scratch_shapes=[
            pltpu.VMEM((MLA_HEADS, n, MLA_SLOT), BF16),
            pltpu.VMEM((MLA_HEADS, n, MLA_SLOT), BF16),
            pltpu.VMEM((MLA_HEADS, MLA_V, n), BF16),
            pltpu.VMEM((n, width), F32),
            pltpu.VMEM((MLA_HEADS, MLA_V, ROW_TILE), F32),
            pltpu.VMEM((2, n, ROW_TILE), F32),
        ],
        compiler_params=pltpu.CompilerParams(
            dimension_semantics=("arbitrary",), vmem_limit_bytes=_vmem_limit(est)),
        name="mla_mixer",
    )(u, w_mla, gq, gkv, wuq2, wukn, wvt, cos_t, sin_t)


def _hgrn_kernel(u_ref, w_ref, lb_ref, gn_ref, out_ref,
                 o_s, st_s, q_s, v_s, gate_s, k_s, b_s, dt_s, zf_s, tot_s, *, n_lat, n_out):
    n = u_ref.shape[1]
    kw = HG_HEADS * HG_DK
    vw = HG_HEADS * HG_DV
    c_f = kw + vw
    c_gate = c_f + 2 * kw
    n_tiles = n // ROW_TILE
    lat_tiles = n_lat // ROW_TILE
    ctx_tiles = n_tiles - lat_tiles
    n_chunk = ROW_TILE // HG_CHUNK
    n_grp = kw // HG_GROUP
    heads_per_grp = HG_GROUP // HG_DK
    stack = heads_per_grp * HG_CHUNK

    ti = lax.broadcasted_iota(jnp.int32, (HG_CHUNK + 16, HG_CHUNK), 0)
    si = lax.broadcasted_iota(jnp.int32, (HG_CHUNK + 16, HG_CHUNK), 1)
    tri = tuple(jnp.where(((ti < HG_CHUNK) & m) | (ti == HG_CHUNK), 1.0, 0.0).astype(BF16)
                for m in (si <= ti, si >= ti))
    hr = lax.broadcasted_iota(jnp.int32, (stack, HG_GROUP), 0) // HG_CHUNK
    hc = lax.broadcasted_iota(jnp.int32, (stack, HG_GROUP), 1) // HG_DK
    head_mask = jnp.where(hr == hc, 1.0, 0.0).astype(BF16)
    hl = lax.broadcasted_iota(jnp.int32, (1, HG_GROUP // 2), 1) // HG_DK
    half_mask = (jnp.where(hl == 0, 1.0, 0.0), jnp.where(hl == 1, 1.0, 0.0))
    at = lax.broadcasted_iota(jnp.int32, (HG_CHUNK, stack), 0)
    as_ = lax.broadcasted_iota(jnp.int32, (HG_CHUNK, stack), 1) % HG_CHUNK
    causal = (as_ <= at, as_ >= at)
    hq = HG_GROUP // 2
    zq = jnp.zeros((hq, hq), BF16)
    ref_row = (HG_CHUNK // 2 - 1, HG_CHUNK // 2)
    end_row = (HG_CHUNK - 1, 0)

    lb_all = lb_ref[...]

    qvg_tile = 768 if n % 768 == 0 else ROW_TILE

    def qvg(t):
        rows = slice(t * qvg_tile, (t + 1) * qvg_tile)
        ut = u_ref[0, rows, :]
        qv = _dot(ut, w_ref[:, 0:c_f])
        q_s[rows, :] = qv[:, 0:kw]
        v_s[rows, :] = qv[:, kw:c_f].astype(BF16)
        gate_s[rows, :] = _silu(_dot(ut, w_ref[:, c_gate:c_gate + vw])).astype(BF16)

    def tiles_of(step):
        t_fwd = jnp.where(step < ctx_tiles, lat_tiles + step, step - ctx_tiles)
        t_bwd = n_tiles - 1 - step
        return (t_fwd, t_bwd)

    def precompute(step, slot):
        tl = tiles_of(step)

        def logits():
            for d in range(2):
                rows = pl.ds(pl.multiple_of(tl[d] * ROW_TILE, ROW_TILE), ROW_TILE)
                zf_s[d] = _dot(u_ref[0, rows, :], w_ref[:, c_f + d * kw:c_f + (d + 1) * kw])

        def gates(cl):
            r = slice(cl * HG_CHUNK, (cl + 1) * HG_CHUNK)
            for d in range(2):
                idx = slot * 2 + d
                zf = zf_s[d, r, :]
                lb = lb_all[:, d * kw:(d + 1) * kw]
                e = jnp.exp(-jnp.abs(zf))
                log_sig = jnp.minimum(zf, 0.0) - jnp.log(1.0 + e)
                a = jnp.log(lb)
                c = jnp.log(1.0 - lb) + log_sig
                log_f = jnp.maximum(a, c) + jnp.log(1.0 + jnp.exp(-jnp.abs(a - c)))
                rcp = 1.0 / (1.0 + e)
                k_s[idx, r, :] = (1.0 - lb) * jnp.where(zf >= 0.0, e * rcp, rcp)
                g_hi, g_lo = _split2(log_f)
                cum = (_dot(tri[d], g_hi) + _dot(tri[d], g_lo)) * LOG2_E
                b_s[idx, r, :] = cum[0:HG_CHUNK]
                tot_s[d, cl:cl + 1, :] = cum[HG_CHUNK:HG_CHUNK + 1]

        def decays():
            for d in range(2):
                dt_s[slot * 2 + d] = jnp.exp2(tot_s[d]).T

        return [logits] + [functools.partial(gates, cl) for cl in range(n_chunk)] + [decays]

    def chunks(step, slot, between=()):
        tl = tiles_of(step)
        between = list(between)
        for cidx in range(n_chunk):
            for d in range(2):
                idx = slot * 2 + d
                cl = cidx if d == 0 else n_chunk - 1 - cidx
                c0 = cl * HG_CHUNK
                rows = slice(c0, c0 + HG_CHUNK)
                grow = pl.ds(pl.multiple_of(tl[d] * ROW_TILE + c0, HG_CHUNK), HG_CHUNK)
                for g in range(n_grp):
                    lanes = slice(g * HG_GROUP, (g + 1) * HG_GROUP)
                    q = q_s[grow, lanes]
                    v = v_s[grow, lanes]
                    k = k_s[idx, rows, lanes]
                    b = b_s[idx, rows, lanes]
                    b_ref = b_s[idx, c0 + ref_row[d]:c0 + ref_row[d] + 1, lanes]
                    b_end = b_s[idx, c0 + end_row[d]:c0 + end_row[d] + 1, lanes]
                    qd = (q * jnp.exp2(b - b_ref)).astype(BF16)
                    qs = (q * jnp.exp2(b)).astype(BF16)
                    kd = (k * jnp.exp2(b_ref - b)).astype(BF16)
                    kr = k * jnp.exp2(b_end - b)
                    kd_st = jnp.concatenate([kd] * heads_per_grp, axis=0) * head_mask
                    v_st = jnp.concatenate([v] * heads_per_grp, axis=0) * head_mask
                    kr_fold = jnp.concatenate(
                        [kr[:, (j // 2) * hq:(j // 2 + 1) * hq] * half_mask[j % 2] for j in range(heads_per_grp)],
                        axis=0)
                    kr_t = kr_fold.T.astype(BF16)
                    amat = jnp.where(causal[d], _dot_nt(qd, kd_st), 0.0).astype(BF16)
                    res = _dot(jnp.concatenate([kr_t, amat], axis=0), v_st)
                    sidx = d * n_grp + g
                    dtc = dt_s[idx, lanes, cl:cl + 1]
                    s0 = st_s[sidx, 0]
                    s1 = st_s[sidx, 1]
                    s_bf = jnp.concatenate(
                        [jnp.concatenate([s0.astype(BF16), zq], axis=1),
                         jnp.concatenate([zq, s1.astype(BF16)], axis=1)], axis=0)
                    o_s[d, grow, lanes] = res[hq:] + _dot(qs, s_bf)
                    st_s[sidx, 0] = s0 * dtc[0:hq] + res[0:hq, 0:hq]
                    st_s[sidx, 1] = s1 * dtc[hq:HG_GROUP] + res[0:hq, hq:HG_GROUP]
            if between:
                between.pop(0)()
        for piece in between:
            piece()

    hi_ = lax.broadcasted_iota(jnp.int32, (vw, vw), 0) // HG_DV
    hj_ = lax.broadcasted_iota(jnp.int32, (vw, vw), 1) // HG_DV
    head_sum = jnp.where(hi_ == hj_, 1.0, 0.0).astype(BF16)

    def finish(t):
        rows = slice(t * ROW_TILE, (t + 1) * ROW_TILE)
        o = o_s[0, rows, :] + o_s[1, rows, :]
        sq_hi, sq_lo = _split2(o * o)
        ms = (_dot(sq_hi, head_sum) + _dot(sq_lo, head_sum)) * (1.0 / HG_DV)
        y = o * lax.rsqrt(ms + RMS_EPS) * gn_ref[...]
        out_ref[0, rows, :] = (y * gate_s[rows, :].astype(F32)).astype(BF16)

    out_tiles = n_out // ROW_TILE
    late_tiles = sorted({0, lat_tiles - 1} & set(range(out_tiles)))
    early_tiles = [t for t in range(out_tiles) if t not in late_tiles]

    def step(i, slot, has_next):
        if has_next:
            pieces = precompute(i + 1, 1 - slot)
            pieces[0]()
            chunks(i, slot, pieces[1:])
        else:
            chunks(i, slot, [functools.partial(finish, t) for t in early_tiles])

    st_s[...] = jnp.zeros_like(st_s)
    tot_s[...] = jnp.zeros_like(tot_s)
    first = precompute(0, 0)
    n_qvg = n // qvg_tile
    per = -(-len(first) // n_qvg)
    for t in range(n_qvg):
        qvg(t)
        for piece in first[t * per:(t + 1) * per]:
            piece()

    def step_pair(j, carry):
        step(2 * j, 0, True)
        step(2 * j + 1, 1, True)
        return carry

    n_pairs = (n_tiles - 1) // 2
    lax.fori_loop(0, n_pairs, step_pair, 0)
    if 2 * n_pairs == n_tiles - 1:
        step(n_tiles - 1, 0, False)
    else:
        step(n_tiles - 2, 0, True)
        step(n_tiles - 1, 1, False)
    for t in late_tiles:
        finish(t)


def _hgrn(u, w_hg, lb, gn, n_lat, n_out):
    bsz, n, d = u.shape
    kw = HG_HEADS * HG_DK
    vw = HG_HEADS * HG_DV
    n_streams = 2 * (kw // HG_GROUP)
    est = (2 * n * d * 2 + 2 * w_hg.size * 2 + 2 * n_out * vw * 2
           + 2 * n * vw * 4 + n_streams * HG_GROUP * HG_GROUP * 2 + n * kw * 4 + n * 2 * vw * 2
           + 8 * ROW_TILE * kw * 4 + 4 * kw * V7X_LANES * 4 + 6 * ROW_TILE * kw * 4)
    full = lambda a: pl.BlockSpec(a.shape, lambda b: (0,) * a.ndim)
    return pl.pallas_call(
        functools.partial(_hgrn_kernel, n_lat=n_lat, n_out=n_out),
        out_shape=jax.ShapeDtypeStruct((bsz, n_out, vw), BF16),
        grid=(bsz,),
        in_specs=[pl.BlockSpec((1, n, d), lambda b: (b, 0, 0)), full(w_hg), full(lb), full(gn)],
        out_specs=pl.BlockSpec((1, n_out, vw), lambda b: (b, 0, 0)),
        scratch_shapes=[
            pltpu.VMEM((2, n, vw), F32),
            pltpu.VMEM((n_streams, 2, HG_GROUP // 2, HG_GROUP // 2), F32),
            pltpu.VMEM((n, kw), F32),
            pltpu.VMEM((n, vw), BF16),
            pltpu.VMEM((n, vw), BF16),
            pltpu.VMEM((4, ROW_TILE, kw), F32),
            pltpu.VMEM((4, ROW_TILE, kw), F32),
            pltpu.VMEM((4, kw, V7X_LANES), F32),
            pltpu.VMEM((2, ROW_TILE, kw), F32),
            pltpu.VMEM((2, V7X_LANES, kw), F32),
        ],
        compiler_params=pltpu.CompilerParams(
            dimension_semantics=("arbitrary",), vmem_limit_bytes=_vmem_limit(est)),
        name="hgrn_mixer",
    )(u, w_hg, lb, gn)


CV_PAD = 8


def _conv_kernel(u_ref, w_ref, cw_ref, cb_ref, out_ref, uu_s, coef_s, *, n_lat, n_out, tile):
    n = u_ref.shape[1]
    cw = cw_ref.shape[-1]
    out_tiles = n_out // tile
    zero = jnp.zeros((CV_PAD, cw), F32)
    uu_s[0:CV_PAD, :] = zero
    uu_s[CV_PAD + n_lat:2 * CV_PAD + n_lat, :] = zero
    uu_s[2 * CV_PAD + n:3 * CV_PAD + n, :] = zero

    def srow(r):
        return r + CV_PAD + (CV_PAD if r >= n_lat else 0)

    def pieces(t):
        r0, r1 = t * tile, (t + 1) * tile
        cand = ((r0, min(r1, n_lat)), (max(r0, n_lat), r1))
        return [(a, b) for a, b in cand if a < b]

    w0 = cw_ref[0:1, :]
    w1 = cw_ref[1:2, :]
    w2 = cw_ref[2:3, :]
    bias = cb_ref[...]

    def project(t):
        r0 = t * tile
        z = _dot(u_ref[0, r0:r0 + tile, :], w_ref[...])
        uu = z[:, 2 * cw:3 * cw] * z[:, 0:cw]
        for a, b in pieces(t):
            uu_s[srow(a):srow(a) + (b - a), :] = uu[a - r0:b - r0]
        coef_s[r0:r0 + tile, :] = z[:, cw:2 * cw] * _silu(z[:, 3 * cw:4 * cw])

    def emit(t):
        for a, b in pieces(t):
            o, m = srow(a), b - a
            conv = uu_s[o - 1:o - 1 + m, :] * w0 + uu_s[o:o + m, :] * w1 + uu_s[o + 1:o + 1 + m, :] * w2 + bias
            out_ref[0, a:b, :] = (coef_s[a:b, :] * conv).astype(BF16)

    for t in range(out_tiles):
        project(t)
        if t >= 1:
            emit(t - 1)
    emit(out_tiles - 1)


def _conv(u, w_cv, conv_w, conv_b, n_lat, n_out):
    bsz, n, d = u.shape
    cw = conv_w.shape[-1]
    tile = next(t for t in (768, 1024, ROW_TILE) if n_out % t == 0)
    est = (2 * n * d * 2 + 2 * w_cv.size * 2 + 2 * n_out * cw * 2 + (n + 3 * CV_PAD) * cw * 4 + n * cw * 4
           + 2 * tile * 4 * cw * 4)
    full = lambda a: pl.BlockSpec(a.shape, lambda b: (0,) * a.ndim)
    return pl.pallas_call(
        functools.partial(_conv_kernel, n_lat=n_lat, n_out=n_out, tile=tile),
        out_shape=jax.ShapeDtypeStruct((bsz, n_out, cw), BF16),
        grid=(bsz,),
        in_specs=[pl.BlockSpec((1, n, d), lambda b: (b, 0, 0)), full(w_cv), full(conv_w), full(conv_b)],
        out_specs=pl.BlockSpec((1, n_out, cw), lambda b: (b, 0, 0)),
        scratch_shapes=[pltpu.VMEM((n + 3 * CV_PAD, cw), F32), pltpu.VMEM((n, cw), F32)],
        compiler_params=pltpu.CompilerParams(
            dimension_semantics=("arbitrary",), vmem_limit_bytes=_vmem_limit(est)),
        name="conv_mixer",
    )(u, w_cv, conv_w, conv_b)


def _merge_kernel(u_ref, ym_ref, yh_ref, yc_ref, h_ref, modb_ref, modc_ref, wg_ref, wb_ref, wo_ref,
                  nmodb_ref, nmodc_ref, ng_ref, *out_refs, n_lat, tm, last):
    d = h_ref.shape[-1]
    sub = MERGE_SUB if tm % MERGE_SUB == 0 else tm

    def project(s):
        rows = slice(s * sub, (s + 1) * sub)
        u = u_ref[0, rows, :]
        hw = d // 2
        halves = []
        for c0 in (0, hw):
            acc = None
            for i, y_ref in enumerate((ym_ref, yh_ref, yc_ref)):
                gate_i = _sigmoid(_dot(u, wg_ref[:, i * d + c0:i * d + c0 + hw]))
                term = gate_i * _dot(y_ref[0, rows, :], wb_ref[i, :, c0:c0 + hw])
                acc = term if acc is None else acc + term
            halves.append(acc.astype(BF16))
        return _dot(jnp.concatenate(halves, axis=1), wo_ref[...])

    def finish(s, out):
        rows = slice(s * sub, (s + 1) * sub)
        row = pl.program_id(1) * tm + s * sub + lax.broadcasted_iota(jnp.int32, (sub, 1), 0)
        is_lat = row < n_lat
        gate = jnp.where(is_lat, modb_ref[0][:, 2 * d:3 * d], modc_ref[0][:, 2 * d:3 * d])
        hn = h_ref[0, rows, :] + gate * out
        if last:
            o_ref, = out_refs
            o_ref[0, rows, :] = hn * lax.rsqrt(jnp.mean(hn * hn, axis=-1, keepdims=True) + RMS_EPS) * ng_ref[...]
        else:
            o_ref, un_ref = out_refs
            o_ref[0, rows, :] = hn
            un_ref[0, rows, :] = _mod_norm(hn, ng_ref[...], nmodb_ref[0], nmodc_ref[0], is_lat)

    pending = None
    for s in range(tm // sub):
        out = project(s)
        if pending is not None:
            finish(*pending)
        pending = (s, out)
    finish(*pending)


def _merge(u, ym, yh, yc, h, mod3, w_gate, w_branch, w_out, next_mod3, next_g, n_lat, tm, last):
    bsz, n, d = h.shape
    bw = ym.shape[-1]
    n_out = n_lat if last else n
    ctx_row = bsz
    tok = lambda w: pl.BlockSpec((1, tm, w), lambda b, j: (b, j, 0))
    modb = pl.BlockSpec((1, 1, 3 * d), lambda b, j: (b, 0, 0))
    modc = pl.BlockSpec((1, 1, 3 * d), lambda b, j: (ctx_row, 0, 0))
    full = lambda a: pl.BlockSpec(a.shape, lambda b, j: (0,) * a.ndim)
    once = lambda a: pl.BlockSpec(a.shape, lambda b, j: (0,) * a.ndim, pipeline_mode=pl.Buffered(1))
    est = (2 * tm * (d * 2 + 3 * bw * 2 + 2 * d * 4 + d * 2) + (w_gate.size + w_branch.size + w_out.size) * 2
           + 6 * tm * d * 4)
    out_shape = [jax.ShapeDtypeStruct((bsz, n_out, d), F32)]
    out_specs = [tok(d)]
    if not last:
        out_shape.append(jax.ShapeDtypeStruct((bsz, n_out, d), BF16))
        out_specs.append(tok(d))
    return pl.pallas_call(
        functools.partial(_merge_kernel, n_lat=n_lat, tm=tm, last=last),
        out_shape=out_shape,
        grid=(bsz, n_out // tm),
        in_specs=[
            tok(d), tok(bw), tok(bw), tok(bw), tok(d), modb, modc,
            once(w_gate), once(w_branch), once(w_out), modb, modc, full(next_g),
        ],
        out_specs=out_specs,
        compiler_params=pltpu.CompilerParams(
            dimension_semantics=("arbitrary", "arbitrary"), vmem_limit_bytes=_vmem_limit(est)),
        name="merge_out",
    )(u, ym, yh, yc, h, mod3, mod3, w_gate, w_branch, w_out, next_mod3, next_mod3, next_g)


def _rope_tables(n_lat, n_ctx):
    pairs = MLA_ROPE // 4
    rows = n_lat // GRID_W
    row_id = np.repeat(np.arange(rows, dtype=np.float32), GRID_W)
    col_id = np.tile(np.arange(GRID_W, dtype=np.float32), rows)
    inv_freq = jnp.power(ROPE_BASE, -jnp.arange(pairs, dtype=F32) / pairs)
    ang = jnp.stack([row_id[:, None] * inv_freq, col_id[:, None] * inv_freq], axis=1)
    ang = jnp.broadcast_to(ang[:, :, None, :], (n_lat, 2, 2, pairs)).reshape(n_lat, MLA_ROPE)
    pad_hi = MLA_SLOT - MLA_NOPE - MLA_ROPE
    cos = jnp.concatenate([jnp.ones((n_lat, MLA_NOPE), F32), jnp.cos(ang), jnp.ones((n_lat, pad_hi), F32)], axis=1)
    sin = jnp.concatenate([jnp.zeros((n_lat, MLA_NOPE), F32), jnp.sin(ang), jnp.zeros((n_lat, pad_hi), F32)], axis=1)
    cos = jnp.concatenate([cos, jnp.ones((n_ctx, MLA_SLOT), F32)], axis=0)
    sin = jnp.concatenate([sin, jnp.zeros((n_ctx, MLA_SLOT), F32)], axis=0)
    return cos, sin


def _rotate_half_cols(w):
    pairs = MLA_ROPE // 4
    ws = w.reshape(w.shape[:-1] + (2, 2, pairs))
    return jnp.stack([-ws[..., 1, :], ws[..., 0, :]], axis=-2).reshape(w.shape)


def _mla_weights(w_in_l, w_uq_l, w_ukv_l, q_rank, kv_rank):
    d = w_in_l.shape[0]
    width = MLA_HEADS * MLA_V
    pad_hi = MLA_SLOT - MLA_NOPE - MLA_ROPE
    c_kr = q_rank + kv_rank
    w_kr = w_in_l[:, c_kr:c_kr + MLA_ROPE]
    slot = lambda w: jnp.concatenate([jnp.zeros((d, MLA_NOPE), w.dtype), w, jnp.zeros((d, pad_hi), w.dtype)], axis=1)
    w_mla = jnp.concatenate([w_in_l[:, 0:c_kr], slot(w_kr), slot(_rotate_half_cols(w_kr)),
                             w_in_l[:, c_kr + MLA_ROPE:c_kr + MLA_ROPE + width]], axis=1)
    uq = w_uq_l.reshape(q_rank, MLA_HEADS, MLA_NOPE + MLA_ROPE)
    uq_n, uq_r = uq[..., :MLA_NOPE], uq[..., MLA_NOPE:]
    zn = jnp.zeros_like(uq_n)
    zp = jnp.zeros((q_rank, MLA_HEADS, pad_hi), uq.dtype)
    main = jnp.concatenate([uq_n, uq_r, zp], axis=-1).reshape(q_rank, MLA_HEADS * MLA_SLOT)
    part = jnp.concatenate([zn, _rotate_half_cols(uq_r), zp], axis=-1).reshape(q_rank, MLA_HEADS * MLA_SLOT)
    wuq2 = jnp.concatenate([main, part], axis=1)
    ukv = w_ukv_l.reshape(kv_rank, MLA_HEADS, MLA_NOPE + MLA_V)
    kn = ukv[..., :MLA_NOPE]
    wukn = jnp.concatenate([kn, jnp.zeros((kv_rank, MLA_HEADS, MLA_SLOT - MLA_NOPE), kn.dtype)], axis=-1)
    wukn = wukn.reshape(kv_rank, MLA_HEADS * MLA_SLOT)
    wvt = ukv[..., MLA_NOPE:].reshape(kv_rank, width).T
    return w_mla, wuq2, wukn, wvt


def kernel(x, c, ctx, c_ctx, ada_w, ada_b, norm_g, w_in, mla_q_norm_g, mla_kv_norm_g, mla_w_uq, mla_w_ukv,
           hg_lb_logits, hg_norm_g, conv_w, conv_b, w_branch, w_out, final_norm_g):
    bsz, n_lat, d = x.shape
    n_ctx = ctx.shape[1]
    n = n_lat + n_ctx
    depth = w_in.shape[0]
    q_rank = mla_q_norm_g.shape[-1]
    kv_rank = mla_kv_norm_g.shape[-1]
    width = MLA_HEADS * MLA_V
    kw = HG_HEADS * HG_DK
    vw = HG_HEADS * HG_DV
    cw = conv_w.shape[-1]
    assert n_lat % ROW_TILE == 0 and n_ctx % ROW_TILE == 0 and n_lat % GRID_W == 0
    assert width == vw == cw == w_branch.shape[2]

    pad = (-(bsz + 1)) % 8
    cs = jnp.concatenate([c, c_ctx[None, :], jnp.zeros((pad, d), c.dtype)], axis=0).astype(F32)
    mod = _modulation(cs, ada_w.astype(F32), ada_b.astype(F32))
    mod3 = [mod[l].reshape(mod.shape[1], 1, 3 * d) for l in range(depth)]
    lb_all = _lower_bounds(hg_lb_logits)
    cos_t, sin_t = _rope_tables(n_lat, n_ctx)

    c_hg = q_rank + kv_rank + MLA_ROPE + width
    c_cv = c_hg + 2 * kw + vw + kw + vw
    c_gate = c_cv + 4 * cw
    w_in_b = w_in.astype(BF16)
    w_branch_b = w_branch.astype(BF16)
    w_out_b = w_out.astype(BF16)
    norm_gs = [norm_g[l].reshape(1, d).astype(F32) for l in range(depth)]
    final_g = final_norm_g.reshape(1, d).astype(F32)

    tm_merge = 768 if n % 768 == 0 else ROW_TILE
    tm_last = 1024 if n_lat % 1024 == 0 else ROW_TILE
    h, u = _entry(x, ctx, mod3[0], norm_gs[0], ROW_TILE)
    for l in range(depth):
        last = l == depth - 1
        n_out = n_lat if last else n
        w_mla, wuq2, wukn, wvt = _mla_weights(w_in_b[l], mla_w_uq[l].astype(BF16), mla_w_ukv[l].astype(BF16),
                                              q_rank, kv_rank)
        y_mla = _mla(u, w_mla, mla_q_norm_g[l].reshape(1, q_rank).astype(F32),
                     mla_kv_norm_g[l].reshape(1, kv_rank).astype(F32), wuq2, wukn, wvt, cos_t, sin_t, n_lat, n_out)
        y_hg = _hgrn(u, w_in_b[l, :, c_hg:c_cv], lb_all[l].reshape(1, 2 * kw),
                     hg_norm_g[l].reshape(1, vw).astype(F32), n_lat, n_out)
        y_cv = _conv(u, w_in_b[l, :, c_cv:c_gate], conv_w[l].astype(F32), conv_b[l].reshape(1, cw).astype(F32),
                     n_lat, n_out)
        if last:
            h, = _merge(u, y_mla, y_hg, y_cv, h, mod3[l], w_in_b[l, :, c_gate:], w_branch_b[l], w_out_b[l],
                        mod3[l], final_g, n_lat, tm_last, True)
        else:
            h, u = _merge(u, y_mla, y_hg, y_cv, h, mod3[l], w_in_b[l, :, c_gate:], w_branch_b[l], w_out_b[l],
                          mod3[l + 1], norm_gs[l + 1], n_lat, tm_merge, False)
    return h
```
